```python
import math
import jax, jax.numpy as jnp
from jax import lax
import numpy as np

D_MODEL = 2048
BATCH = 16
SEQ = 2048
DEPTH = 1

ATTN_WIDTH = D_MODEL // 2
SSM_WIDTH = D_MODEL - ATTN_WIDTH
N_HEADS = 16
HEAD_DIM = ATTN_WIDTH // N_HEADS
MOBA_BLOCK = 256
MOBA_TOPK = 3
Q_CHUNK = 8
SSM_GROUP = 16
SSM_GROUPS = SSM_WIDTH // SSM_GROUP
SSM_STATE = 64
D_FF = 4 * D_MODEL
PROJ_WIDTH = 3 * ATTN_WIDTH + SSM_WIDTH
EPS = 1e-6
DT_MIN = 1e-3
DT_MAX = 1e-1
NEG = -1e30

kernel_name = "hymba_moba_s5_hybrid_block"


def rms_norm_f32(x, g):
    xf = x.astype(jnp.float32)
    y = xf * lax.rsqrt(jnp.mean(xf * xf, axis=-1, keepdims=True) + EPS)
    return y * g.astype(jnp.float32)


def rms_norm(x, g):
    return rms_norm_f32(x, g).astype(x.dtype)


def alibi_slopes(n_heads):
    return jnp.exp2(-8.0 * (jnp.arange(n_heads, dtype=jnp.float32) + 1.0) / n_heads)


def moba_attention(q, k, v):
    B_, H, L, Dh = q.shape
    nb = -(-L // MOBA_BLOCK)
    pad = nb * MOBA_BLOCK - L
    k_blk = jnp.pad(k, ((0, 0), (0, 0), (0, pad), (0, 0))).reshape(B_, H, nb, MOBA_BLOCK, Dh)
    v_blk = jnp.pad(v, ((0, 0), (0, 0), (0, pad), (0, 0))).reshape(B_, H, nb, MOBA_BLOCK, Dh)
    k_mean = jnp.mean(k_blk, axis=3)
    top_k = min(MOBA_TOPK, nb - 1)
    slopes = alibi_slopes(H)[None, :, None, None]
    scale = Dh ** -0.5
    n_chunks = L // Q_CHUNK
    q_c = q.reshape(B_, H, n_chunks, Q_CHUNK, Dh).transpose(2, 0, 1, 3, 4)
    b_idx = jnp.arange(B_)[:, None, None, None]
    h_idx = jnp.arange(H)[None, :, None, None]
    blk_pos = jnp.arange(MOBA_BLOCK)
    blk_ids = jnp.arange(nb)

    def chunk_fn(args):
        qc, c = args
        t = c * Q_CHUNK + jnp.arange(Q_CHUNK)
        own = (c * Q_CHUNK) // MOBA_BLOCK
        k_own = lax.dynamic_index_in_dim(k_blk, own, axis=2, keepdims=False)
        v_own = lax.dynamic_index_in_dim(v_blk, own, axis=2, keepdims=False)
        s_own = own * MOBA_BLOCK + blk_pos
        dist_own = (t[:, None] - s_own[None, :]).astype(jnp.float32)
        logit_own = jnp.einsum('bhqd,bhkd->bhqk', qc, k_own) * scale - slopes * dist_own
        logit_own = jnp.where(s_own[None, :] <= t[:, None], logit_own, NEG)
        if top_k == 0:
            p = jax.nn.softmax(logit_own, axis=-1)
            return jnp.einsum('bhqk,bhkd->bhqd', p, v_own)
        gate = jnp.einsum('bhqd,bhnd->bhqn', qc, k_mean)
        gate = jnp.where(blk_ids < own, gate, NEG)
        _, sel = lax.top_k(gate, top_k)
        valid = jnp.arange(top_k) < jnp.minimum(own, top_k)
        k_sel = k_blk[b_idx, h_idx, sel]
        v_sel = v_blk[b_idx, h_idx, sel]
        s_pos = sel[..., None] * MOBA_BLOCK + blk_pos
        dist_past = (t[:, None, None] - s_pos).astype(jnp.float32)
        logit_past = jnp.einsum('bhqd,bhqjkd->bhqjk', qc, k_sel) * scale - slopes[..., None] * dist_past
        logit_past = jnp.where(valid[:, None], logit_past, NEG)
        n_past = top_k * MOBA_BLOCK
        logits = jnp.concatenate([logit_past.reshape(B_, H, Q_CHUNK, n_past), logit_own], axis=-1)
        p = jax.nn.softmax(logits, axis=-1)
        p_past = p[..., :n_past].reshape(B_, H, Q_CHUNK, top_k, MOBA_BLOCK)
        return (jnp.einsum('bhqjk,bhqjkd->bhqd', p_past, v_sel)
                + jnp.einsum('bhqk,bhkd->bhqd', p[..., n_past:], v_own))

    out = lax.map(chunk_fn, (q_c, jnp.arange(n_chunks)))
    return out.transpose(1, 2, 0, 3, 4).reshape(B_, H, L, Dh)


def s5_ssm(u, lambda_re, lambda_im, log_dt, b_re, b_im, c_re, c_im, d_skip):
    B_, L, _ = u.shape
    f32 = jnp.float32
    uf = u.astype(f32).reshape(B_, L, SSM_GROUPS, SSM_GROUP)
    dt = jnp.exp(log_dt.astype(f32))[:, None]
    lr = lambda_re.astype(f32)
    li = lambda_im.astype(f32)
    mag = jnp.exp(lr * dt)
    ab_re = mag * jnp.cos(li * dt)
    ab_im = mag * jnp.sin(li * dt)
    den = lr * lr + li * li
    nr = ab_re - 1.0
    f_re = (nr * lr + ab_im * li) / den
    f_im = (ab_im * lr - nr * li) / den
    br = b_re.astype(f32)
    bi = b_im.astype(f32)
    bb_re = f_re[..., None] * br - f_im[..., None] * bi
    bb_im = f_re[..., None] * bi + f_im[..., None] * br
    bu_re = jnp.einsum('blgh,gph->lbgp', uf, bb_re)
    bu_im = jnp.einsum('blgh,gph->lbgp', uf, bb_im)
    a_re = jnp.broadcast_to(ab_re, (L, 1, SSM_GROUPS, SSM_STATE))
    a_im = jnp.broadcast_to(ab_im, (L, 1, SSM_GROUPS, SSM_STATE))

    def combine(e_i, e_j):
        air, aii, bir, bii = e_i
        ajr, aji, bjr, bji = e_j
        return (ajr * air - aji * aii,
                ajr * aii + aji * air,
                ajr * bir - aji * bii + bjr,
                ajr * bii + aji * bir + bji)

    _, _, s_re, s_im = lax.associative_scan(combine, (a_re, a_im, bu_re, bu_im), axis=0)
    y = (jnp.einsum('lbgp,ghp->blgh', s_re, c_re.astype(f32))
         - jnp.einsum('lbgp,ghp->blgh', s_im, c_im.astype(f32)))
    y = y + d_skip.astype(f32).reshape(SSM_GROUPS, SSM_GROUP) * uf
    return y.reshape(B_, L, SSM_WIDTH)


def setup_inputs(seed: int = 0) -> dict:
    key = jax.random.key(seed)
    ks = jax.random.split(key, 24)
    f32 = jnp.float32
    nrm = lambda k, shape, s: jax.random.normal(k, shape, f32) * s
    gain = lambda k, shape: 1.0 + 0.02 * jax.random.normal(k, shape, f32)
    n_idx = jnp.arange(SSM_STATE, dtype=f32)
    return {
        "x": jax.random.normal(ks[0], (BATCH, SEQ, D_MODEL), f32),
        "norm1_gain": gain(ks[1], (DEPTH, D_MODEL)),
        "w_in": nrm(ks[2], (DEPTH, D_MODEL, PROJ_WIDTH), D_MODEL ** -0.5),
        "q_norm_gain": gain(ks[3], (DEPTH, HEAD_DIM)),
        "k_norm_gain": gain(ks[4], (DEPTH, HEAD_DIM)),
        "attn_out_gain": gain(ks[5], (DEPTH, ATTN_WIDTH)),
        "lambda_re": -0.5 + 0.01 * jax.random.normal(ks[6], (DEPTH, SSM_GROUPS, SSM_STATE), f32),
        "lambda_im": math.pi * n_idx + 0.01 * jax.random.normal(ks[7], (DEPTH, SSM_GROUPS, SSM_STATE), f32),
        "log_dt": jax.random.uniform(ks[8], (DEPTH, SSM_GROUPS), f32, math.log(DT_MIN), math.log(DT_MAX)),
        "b_re": nrm(ks[9], (DEPTH, SSM_GROUPS, SSM_STATE, SSM_GROUP), (2.0 * SSM_GROUP) ** -0.5),
        "b_im": nrm(ks[10], (DEPTH, SSM_GROUPS, SSM_STATE, SSM_GROUP), (2.0 * SSM_GROUP) ** -0.5),
        "c_re": nrm(ks[11], (DEPTH, SSM_GROUPS, SSM_GROUP, SSM_STATE), (2.0 * SSM_STATE) ** -0.5),
        "c_im": nrm(ks[12], (DEPTH, SSM_GROUPS, SSM_GROUP, SSM_STATE), (2.0 * SSM_STATE) ** -0.5),
        "d_skip": nrm(ks[13], (DEPTH, SSM_WIDTH), 1.0),
        "w_glu": nrm(ks[14], (DEPTH, SSM_WIDTH, SSM_WIDTH), SSM_WIDTH ** -0.5),
        "b_glu": nrm(ks[15], (DEPTH, SSM_WIDTH), 0.02),
        "ssm_out_gain": gain(ks[16], (DEPTH, SSM_WIDTH)),
        "w_out": nrm(ks[17], (DEPTH, D_MODEL, D_MODEL), D_MODEL ** -0.5),
        "norm2_gain": gain(ks[18], (DEPTH, D_MODEL)),
        "w_ff1": nrm(ks[19], (DEPTH, D_MODEL, D_FF), D_MODEL ** -0.5),
        "w_ff2": nrm(ks[20], (DEPTH, D_FF, D_MODEL), D_FF ** -0.5),
    }


def reference(x, norm1_gain, w_in, q_norm_gain, k_norm_gain, attn_out_gain,
              lambda_re, lambda_im, log_dt, b_re, b_im, c_re, c_im, d_skip,
              w_glu, b_glu, ssm_out_gain, w_out, norm2_gain, w_ff1, w_ff2):
    B_, L, _ = x.shape

    def to_heads(t):
        return t.reshape(B_, L, N_HEADS, HEAD_DIM).transpose(0, 2, 1, 3)

    for i in range(DEPTH):
        h = rms_norm(x, norm1_gain[i])
        proj = jnp.einsum('bld,de->ble', h, w_in[i])
        q, k, v, u = jnp.split(proj, [ATTN_WIDTH, 2 * ATTN_WIDTH, 3 * ATTN_WIDTH], axis=-1)
        qh = rms_norm_f32(to_heads(q), q_norm_gain[i])
        kh = rms_norm_f32(to_heads(k), k_norm_gain[i])
        vh = to_heads(v).astype(jnp.float32)
        attn = moba_attention(qh, kh, vh)
        attn = attn.transpose(0, 2, 1, 3).reshape(B_, L, ATTN_WIDTH)
        ssm = s5_ssm(u, lambda_re[i], lambda_im[i], log_dt[i], b_re[i], b_im[i],
                     c_re[i], c_im[i], d_skip[i])
        ssm = jax.nn.gelu(ssm)
        ssm = ssm * jax.nn.sigmoid(jnp.einsum('blc,ce->ble', ssm, w_glu[i].astype(jnp.float32))
                                   + b_glu[i].astype(jnp.float32))
        mixed = jnp.concatenate([rms_norm_f32(attn, attn_out_gain[i]),
                                 rms_norm_f32(ssm, ssm_out_gain[i])], axis=-1).astype(x.dtype)
        x = x + jnp.einsum('blc,cd->bld', mixed, w_out[i])
        h2 = rms_norm(x, norm2_gain[i])
        ff = jnp.square(jax.nn.relu(jnp.einsum('bld,df->blf', h2, w_ff1[i])))
        x = x + jnp.einsum('blf,fd->bld', ff, w_ff2[i])
    return x
```

```python
import functools
import math

import jax
import jax.numpy as jnp
from jax import lax
from jax.experimental import pallas as pl
from jax.experimental.pallas import tpu as pltpu

f32 = jnp.float32
bf16 = jnp.bfloat16

EPS = 1e-6
NEG = -1e30
HEAD_DIM = 64
MOBA_BLOCK = 256
MOBA_TOPK = 3
SSM_GROUP = 16
SSM_STATE = 64
SSM_CHUNK = 128
VMEM_LIMIT = 56 * 1024 * 1024
HIGHEST = lax.Precision.HIGHEST


def _params(sem):
    return pltpu.CompilerParams(dimension_semantics=sem, vmem_limit_bytes=VMEM_LIMIT)


def _in_proj_kernel(x_ref, g1_ref, w_ref, gq_ref, gk_ref, q_ref, k_ref, v_ref, u_ref, h_scr):
    j = pl.program_id(2)
    tm = x_ref.shape[1]

    @pl.when(j == 0)
    def _():
        x = x_ref[0]
        ms = jnp.mean(x * x, axis=-1, keepdims=True)
        h_scr[...] = (x * lax.rsqrt(ms + EPS) * g1_ref[...]).astype(bf16)

    t = lax.dot_general(w_ref[...], h_scr[...], (((1,), (1,)), ((), ())), preferred_element_type=f32)

    def head_norm(g_ref):
        n_heads = t.shape[0] // HEAD_DIM
        t3 = t.reshape(n_heads, HEAD_DIM, tm)
        ms = jnp.mean(t3 * t3, axis=1, keepdims=True)
        return (t3 * lax.rsqrt(ms + EPS) * g_ref[...][None]).reshape(t.shape)

    @pl.when(j == 0)
    def _():
        q_ref[0] = head_norm(gq_ref)

    @pl.when(j == 1)
    def _():
        k_ref[0] = head_norm(gk_ref)

    @pl.when(j == 2)
    def _():
        v_ref[0] = t.astype(bf16)

    @pl.when(j == 3)
    def _():
        u_ref[0] = t


def _in_proj(x, g1, w_in_t, gq, gk, width, tm=512):
    B, L, D = x.shape
    assert w_in_t.shape == (4 * width, D) and L % tm == 0
    out_spec = pl.BlockSpec((1, width, tm), lambda b, i, j: (b, 0, i))
    return pl.pallas_call(
        _in_proj_kernel,
        grid=(B, L // tm, 4),
        in_specs=[
            pl.BlockSpec((1, tm, D), lambda b, i, j: (b, i, 0)),
            pl.BlockSpec((1, D), lambda b, i, j: (0, 0)),
            pl.BlockSpec((width, D), lambda b, i, j: (j, 0)),
            pl.BlockSpec((HEAD_DIM, 1), lambda b, i, j: (0, 0)),
            pl.BlockSpec((HEAD_DIM, 1), lambda b, i, j: (0, 0)),
        ],
        out_specs=[out_spec, out_spec, out_spec, out_spec],
        out_shape=[
            jax.ShapeDtypeStruct((B, width, L), f32),
            jax.ShapeDtypeStruct((B, width, L), f32),
            jax.ShapeDtypeStruct((B, width, L), bf16),
            jax.ShapeDtypeStruct((B, width, L), f32),
        ],
        scratch_shapes=[pltpu.VMEM((tm, D), bf16)],
        compiler_params=_params(("arbitrary", "arbitrary", "arbitrary")),
        name="in_proj",
    )(x, g1, w_in_t, gq, gk)


def _moba_kernel(slope_ref, q_ref, k_ref, v_ref, o_ref, ktok_scr, vblk_scr, sel_scr, bias_scr):
    L = q_ref.shape[2]
    blk = MOBA_BLOCK
    nb = L // blk
    top_k = min(MOBA_TOPK, nb - 1)
    scale = HEAD_DIM ** -0.5
    pair = pl.program_id(1)

    r_i = lax.broadcasted_iota(jnp.int32, (blk, blk), 0)
    c_i = lax.broadcasted_iota(jnp.int32, (blk, blk), 1)
    blk_id = lax.broadcasted_iota(jnp.int32, (nb, blk), 0)

    ktok = k_ref[0].T
    ktok_scr[...] = ktok.astype(bf16)
    kmean = jnp.concatenate(
        [jnp.mean(ktok[j * blk:(j + 1) * blk], axis=0, keepdims=True) for j in range(nb)], axis=0)

    for hh in range(2):
        rows = slice(hh * HEAD_DIM, (hh + 1) * HEAD_DIM)
        slope = slope_ref[pair * 2 + hh]
        bias_off = slope * r_i.astype(f32)
        bias_scr[0] = bias_off
        bias_scr[1] = jnp.where(r_i <= c_i, bias_off, NEG)
        for j in range(nb):
            vblk_scr[j] = v_ref[0, rows, j * blk:(j + 1) * blk]

        def both_heads(a):
            z = jnp.zeros_like(a)
            return jnp.concatenate([a, z] if hh == 0 else [z, a], axis=0)

        for i in range(nb):
            qf = q_ref[0, rows, i * blk:(i + 1) * blk]
            qb = both_heads((qf * scale).astype(bf16))
            s = jnp.dot(ktok_scr[i * blk:(i + 1) * blk, :], qb, preferred_element_type=f32) + bias_scr[1]
            m = jnp.max(s, axis=0, keepdims=True)
            p = jnp.exp(s - m)
            l = jnp.sum(p, axis=0, keepdims=True)
            acc = jnp.dot(vblk_scr[i], p.astype(bf16), preferred_element_type=f32)
            if i > 0:
                gate = jnp.dot(kmean, both_heads(qf), precision=HIGHEST, preferred_element_type=f32)
                rank = jnp.zeros((nb, blk), f32)
                for jp in range(i):
                    row = gate[jp:jp + 1, :]
                    beats = (row > gate) | ((row == gate) & (jp < blk_id))
                    rank = rank + jnp.where(beats, 1.0, 0.0)
                sel_scr[...] = jnp.where((blk_id < i) & (rank < top_k), 1.0, 0.0)

                def body(j, carry, qb=qb, i=i, slope=slope):
                    m, l, acc = carry
                    kj = ktok_scr[pl.ds(pl.multiple_of(j * blk, blk), blk), :]
                    s = jnp.dot(kj, qb, preferred_element_type=f32) + bias_scr[0]
                    selj = sel_scr[pl.ds(j, 1), :] > 0.5
                    cj = slope * jnp.full((1, blk), (j - i) * blk, jnp.int32).astype(f32)
                    mb = jnp.where(selj, jnp.max(s, axis=0, keepdims=True) + cj, NEG)
                    m_new = jnp.maximum(m, mb)
                    alpha = jnp.exp(m - m_new)
                    shift = jnp.where(selj, cj - m_new, NEG)
                    p = jnp.exp(s + shift)
                    l = alpha * l + jnp.sum(p, axis=0, keepdims=True)
                    acc = alpha * acc + jnp.dot(vblk_scr[j], p.astype(bf16), preferred_element_type=f32)
                    return m_new, l, acc

                m, l, acc = lax.fori_loop(0, i, body, (m, l, acc))
            o_ref[0, rows, i * blk:(i + 1) * blk] = acc / l


def _moba(q_t, k_t, v_t):
    B, width, L = q_t.shape
    n_heads = width // HEAD_DIM
    assert n_heads % 2 == 0 and L % MOBA_BLOCK == 0
    nb = L // MOBA_BLOCK
    slopes = jnp.exp2(-8.0 * (jnp.arange(n_heads, dtype=f32) + 1.0) / n_heads)
    spec = pl.BlockSpec((1, 2 * HEAD_DIM, L), lambda b, h: (b, h, 0))
    return pl.pallas_call(
        _moba_kernel,
        grid=(B, n_heads // 2),
        in_specs=[pl.BlockSpec(memory_space=pltpu.SMEM), spec, spec, spec],
        out_specs=spec,
        out_shape=jax.ShapeDtypeStruct((B, width, L), f32),
        scratch_shapes=[
            pltpu.VMEM((L, 2 * HEAD_DIM), bf16),
            pltpu.VMEM((nb, HEAD_DIM, MOBA_BLOCK), bf16),
            pltpu.VMEM((nb, MOBA_BLOCK), f32),
            pltpu.VMEM((2, MOBA_BLOCK, MOBA_BLOCK), f32),
        ],
        compiler_params=_params(("arbitrary", "arbitrary")),
        name="moba",
    )(slopes, q_t, k_t, v_t)


def _gelu_tanh(x):
    return 0.5 * x * (1.0 + jnp.tanh(math.sqrt(2.0 / math.pi) * (x + 0.044715 * (x * x * x))))


def _ssm_kernel(u_ref, lrr_ref, lir_ref, lrc_ref, lic_ref, ldt_ref, btr_ref, bti_ref,
                cr_ref, ci_ref, ctr_ref, cti_ref, dexp_ref, y_ref,
                lhs_scr, m_scr, ws_scr, wo_scr, kt_scr, sp_scr):
    B = u_ref.shape[0]
    L = u_ref.shape[2]
    T = SSM_CHUNK
    P = SSM_STATE
    Hc = SSM_GROUP
    nC = L // T

    for c in range(nC):
        for hp in range(Hc):
            lhs_scr[c * B:(c + 1) * B, hp * T:(hp + 1) * T] = u_ref[:, hp, c * T:(c + 1) * T].astype(bf16)

    dt = jnp.exp(ldt_ref[0])

    def lam_pow(lr, li, n):
        mag = jnp.exp(n * (lr * dt))
        ang = n * (li * dt)
        return mag * jnp.cos(ang), mag * jnp.sin(ang)

    lr_r, li_r = lrr_ref[0], lir_ref[0]
    lr_c, li_c = lrc_ref[0], lic_ref[0]

    ab_re, ab_im = lam_pow(lr_r, li_r, 1.0)
    den = lr_r * lr_r + li_r * li_r
    nr = ab_re - 1.0
    f_re = (nr * lr_r + ab_im * li_r) / den
    f_im = (ab_im * lr_r - nr * li_r) / den
    bb_re = f_re * btr_ref[0] - f_im * bti_ref[0]
    bb_im = f_re * bti_ref[0] + f_im * btr_ref[0]

    c_re, c_im = cr_ref[0], ci_ref[0]
    cb_re = (c_re[:, None, :] * bb_re[None, :, :] - c_im[:, None, :] * bb_im[None, :, :]).reshape(Hc * Hc, P)
    cb_im = (c_re[:, None, :] * bb_im[None, :, :] + c_im[:, None, :] * bb_re[None, :, :]).reshape(Hc * Hc, P)
    d_lane = lax.broadcasted_iota(jnp.int32, (P, T), 1).astype(f32)
    pd_re, pd_im = lam_pow(lr_c, li_c, d_lane)
    kt_scr[...] = (jnp.dot(cb_re, pd_re, precision=HIGHEST, preferred_element_type=f32)
                   - jnp.dot(cb_im, pd_im, precision=HIGHEST, preferred_element_type=f32))

    tau_i = lax.broadcasted_iota(jnp.int32, (T, T), 0)
    t_i = lax.broadcasted_iota(jnp.int32, (T, T), 1)
    causal = t_i >= tau_i

    def toeplitz_rows(hp, carry):
        for h in range(Hc):
            row = kt_scr[pl.ds(h * Hc + hp, 1), :]
            tile = pltpu.roll(jnp.broadcast_to(row, (T, T)), 0, 1, stride=1, stride_axis=0)
            m_scr[pl.ds(pl.multiple_of(hp * T, T), T), h * T:(h + 1) * T] = jnp.where(causal, tile, 0.0).astype(bf16)
        return carry

    lax.fori_loop(0, Hc, toeplitz_rows, 0)

    tau_s = lax.broadcasted_iota(jnp.int32, (T, P), 0).astype(f32)
    pe_re, pe_im = lam_pow(lr_r, li_r, (T - 1.0) - tau_s)
    for hp in range(Hc):
        br, bi = bb_re[hp:hp + 1, :], bb_im[hp:hp + 1, :]
        ws_scr[hp * T:(hp + 1) * T, :] = jnp.concatenate(
            [pe_re * br - pe_im * bi, pe_re * bi + pe_im * br], axis=1).astype(bf16)
    po_re, po_im = lam_pow(lr_c, li_c, d_lane + 1.0)
    ct_re, ct_im = ctr_ref[0], cti_ref[0]
    for h in range(Hc):
        cre, cim = ct_re[:, h:h + 1], ct_im[:, h:h + 1]
        wo_scr[:, h * T:(h + 1) * T] = jnp.concatenate(
            [cre * po_re - cim * po_im, -(cre * po_im + cim * po_re)], axis=0).astype(bf16)

    lhs = lhs_scr[...]
    y = jnp.dot(lhs, m_scr[...], preferred_element_type=f32)
    v = jnp.dot(lhs, ws_scr[...], preferred_element_type=f32)

    a_re, a_im = lam_pow(lr_r, li_r, float(T))
    a1 = jnp.concatenate([a_re, a_re], axis=1)
    a2 = jnp.concatenate([-a_im, a_im], axis=1)
    s = jnp.zeros((B, 2 * P), f32)
    for c in range(nC):
        sp_scr[c * B:(c + 1) * B, :] = s
        s = a1 * s + a2 * pltpu.roll(s, P, 1) + v[c * B:(c + 1) * B, :]

    y = y + jnp.dot(sp_scr[...].astype(bf16), wo_scr[...], preferred_element_type=f32)
    y = y + dexp_ref[0] * lhs.astype(f32)
    y = _gelu_tanh(y)
    for c in range(nC):
        for h in range(Hc):
            y_ref[:, h, c * T:(c + 1) * T] = y[c * B:(c + 1) * B, h * T:(h + 1) * T]


def _ssm(u_t, lambda_re, lambda_im, log_dt, b_re, b_im, c_re, c_im, d_skip):
    B, width, L = u_t.shape
    G, P = lambda_re.shape
    Hc, T = SSM_GROUP, SSM_CHUNK
    assert width == G * Hc and P == SSM_STATE and L % T == 0
    nC = L // T
    row = lambda a: a.reshape(G, 1, P)
    col = lambda a: a.reshape(G, P, 1)
    d_exp = jnp.repeat(d_skip.reshape(G, 1, Hc), T, axis=2)
    args = (u_t, row(lambda_re), row(lambda_im), col(lambda_re), col(lambda_im), log_dt.reshape(G, 1, 1),
            jnp.swapaxes(b_re, 1, 2), jnp.swapaxes(b_im, 1, 2), c_re, c_im,
            jnp.swapaxes(c_re, 1, 2), jnp.swapaxes(c_im, 1, 2), d_exp)
    g3 = lambda shp: pl.BlockSpec((1,) + shp, lambda g: (g, 0, 0))
    u_spec = pl.BlockSpec((B, Hc, L), lambda g: (0, g, 0))
    return pl.pallas_call(
        _ssm_kernel,
        grid=(G,),
        in_specs=[u_spec, g3((1, P)), g3((1, P)), g3((P, 1)), g3((P, 1)), g3((1, 1)),
                  g3((Hc, P)), g3((Hc, P)), g3((Hc, P)), g3((Hc, P)), g3((P, Hc)), g3((P, Hc)),
                  g3((1, Hc * T))],
        out_specs=u_spec,
        out_shape=jax.ShapeDtypeStruct((B, width, L), f32),
        scratch_shapes=[
            pltpu.VMEM((nC * B, Hc * T), bf16),
            pltpu.VMEM((Hc * T, Hc * T), bf16),
            pltpu.VMEM((Hc * T, 2 * P), bf16),
            pltpu.VMEM((2 * P, Hc * T), bf16),
            pltpu.VMEM((Hc * Hc, T), f32),
            pltpu.VMEM((nC * B, 2 * P), f32),
        ],
        compiler_params=_params(("arbitrary",)),
        name="ssm",
    )(*args)


def _mix_kernel(x_ref, a_ref, y_ref, wg_ref, bg_ref, ga_ref, gs_ref, wo_ref, o_ref):
    def rms_rows(t, g_ref):
        return t * lax.rsqrt(jnp.mean(t * t, axis=0, keepdims=True) + EPS) * g_ref[...]

    a_n = rms_rows(a_ref[0], ga_ref)
    yg = y_ref[0]
    z = jnp.dot(wg_ref[...], yg.astype(bf16), preferred_element_type=f32) + bg_ref[...]
    s_n = rms_rows(yg * jax.nn.sigmoid(z), gs_ref)
    mixed_t = jnp.concatenate([a_n, s_n], axis=0).astype(bf16)
    delta = lax.dot_general(mixed_t, wo_ref[...], (((0,), (0,)), ((), ())), preferred_element_type=f32)
    o_ref[0] = x_ref[0] + delta


def _mix(x, attn_t, yg_t, w_glu_t, b_glu, g_attn, g_ssm, w_out, tn=256):
    B, L, D = x.shape
    aw, sw = attn_t.shape[1], yg_t.shape[1]
    assert aw + sw == D and L % tn == 0
    const = lambda shp: pl.BlockSpec(shp, lambda b, i: (0, 0))
    return pl.pallas_call(
        _mix_kernel,
        grid=(B, L // tn),
        in_specs=[
            pl.BlockSpec((1, tn, D), lambda b, i: (b, i, 0)),
            pl.BlockSpec((1, aw, tn), lambda b, i: (b, 0, i)),
            pl.BlockSpec((1, sw, tn), lambda b, i: (b, 0, i)),
            const((sw, sw)), const((sw, 1)), const((aw, 1)), const((sw, 1)), const((D, D)),
        ],
        out_specs=pl.BlockSpec((1, tn, D), lambda b, i: (b, i, 0)),
        out_shape=jax.ShapeDtypeStruct((B, L, D), f32),
        compiler_params=_params(("arbitrary", "arbitrary")),
        name="mix",
    )(x, attn_t, yg_t, w_glu_t, b_glu, g_attn, g_ssm, w_out)


def _ffn_kernel(x_ref, g2_ref, w1_ref, w2_ref, o_ref, h_scr):
    f = pl.program_id(1)

    @pl.when(f == 0)
    def _():
        x = x_ref[...]
        ms = jnp.mean(x * x, axis=-1, keepdims=True)
        h_scr[...] = (x * lax.rsqrt(ms + EPS) * g2_ref[...]).astype(bf16)
        o_ref[...] = x

    a = jnp.dot(h_scr[...], w1_ref[...], preferred_element_type=f32)
    a = jnp.square(jnp.maximum(a, 0.0)).astype(bf16)
    o_ref[...] += jnp.dot(a, w2_ref[...], preferred_element_type=f32)


def _ffn(x2d, g2, w1, w2, tm=512, tf=1024):
    N, D = x2d.shape
    F = w1.shape[1]
    assert N % tm == 0 and F % tf == 0
    return pl.pallas_call(
        _ffn_kernel,
        grid=(N // tm, F // tf),
        in_specs=[
            pl.BlockSpec((tm, D), lambda i, f: (i, 0)),
            pl.BlockSpec((1, D), lambda i, f: (0, 0)),
            pl.BlockSpec((D, tf), lambda i, f: (0, f)),
            pl.BlockSpec((tf, D), lambda i, f: (f, 0)),
        ],
        out_specs=pl.BlockSpec((tm, D), lambda i, f: (i, 0)),
        out_shape=jax.ShapeDtypeStruct((N, D), f32),
        scratch_shapes=[pltpu.VMEM((tm, D), bf16)],
        compiler_params=_params(("arbitrary", "arbitrary")),
        name="ffn",
    )(x2d, g2, w1, w2)


def _layer(x, norm1_gain, w_in, q_norm_gain, k_norm_gain, attn_out_gain, lambda_re, lambda_im, log_dt,
           b_re, b_im, c_re, c_im, d_skip, w_glu, b_glu, ssm_out_gain, w_out, norm2_gain, w_ff1, w_ff2):
    B, L, D = x.shape
    sw = d_skip.shape[0]
    aw = D - sw
    assert aw == sw and w_in.shape == (D, 3 * aw + sw)
    q_t, k_t, v_t, u_t = _in_proj(
        x, norm1_gain.reshape(1, D), w_in.T.astype(bf16),
        q_norm_gain.reshape(HEAD_DIM, 1), k_norm_gain.reshape(HEAD_DIM, 1), aw)
    attn_t = _moba(q_t, k_t, v_t)
    yg_t = _ssm(u_t, lambda_re, lambda_im, log_dt, b_re, b_im, c_re, c_im, d_skip)
    x1 = _mix(x, attn_t, yg_t, w_glu.T.astype(bf16), b_glu.reshape(sw, 1),
              attn_out_gain.reshape(aw, 1), ssm_out_gain.reshape(sw, 1), w_out.astype(bf16))
    out = _ffn(x1.reshape(B * L, D), norm2_gain.reshape(1, D), w_ff1.astype(bf16), w_ff2.astype(bf16))
    return out.reshape(B, L, D)


def kernel(x, norm1_gain, w_in, q_norm_gain, k_norm_gain, attn_out_gain, lambda_re, lambda_im, log_dt, b_re, b_im, c_re, c_im, d_skip, w_glu, b_glu, ssm_out_gain, w_out, norm2_gain, w_ff1, w_ff2):
    per_layer = (norm1_gain, w_in, q_norm_gain, k_norm_gain, attn_out_gain, lambda_re, lambda_im, log_dt,
                 b_re, b_im, c_re, c_im, d_skip, w_glu, b_glu, ssm_out_gain, w_out, norm2_gain, w_ff1, w_ff2)
    for i in range(w_in.shape[0]):
        x = _layer(x, *(p[i] for p in per_layer))
    return x
```

```python
import functools
import math

import jax
import jax.numpy as jnp
from jax import lax
from jax.experimental import pallas as pl
from jax.experimental.pallas import tpu as pltpu

f32 = jnp.float32
bf16 = jnp.bfloat16

EPS = 1e-6
NEG = -1e30
HEAD_DIM = 64
MOBA_BLOCK = 256
MOBA_TOPK = 3
SSM_GROUP = 16
SSM_STATE = 64
SSM_CHUNK = 128
VMEM_LIMIT = 56 * 1024 * 1024
HIGHEST = lax.Precision.HIGHEST


def _params(sem):
    return pltpu.CompilerParams(dimension_semantics=sem, vmem_limit_bytes=VMEM_LIMIT)


def _in_proj_kernel(x_ref, g1_ref, w_ref, gq_ref, gk_ref, q_ref, k_ref, v_ref, u_ref, h_scr):
    j = pl.program_id(2)
    tm = x_ref.shape[1]

    @pl.when(j == 0)
    def _():
        x = x_ref[0]
        ms = jnp.mean(x * x, axis=-1, keepdims=True)
        h_scr[...] = (x * lax.rsqrt(ms + EPS) * g1_ref[...]).astype(bf16)

    t = lax.dot_general(w_ref[...], h_scr[...], (((1,), (1,)), ((), ())), preferred_element_type=f32)

    def head_norm(g_ref):
        n_heads = t.shape[0] // HEAD_DIM
        t3 = t.reshape(n_heads, HEAD_DIM, tm)
        ms = jnp.mean(t3 * t3, axis=1, keepdims=True)
        return (t3 * lax.rsqrt(ms + EPS) * g_ref[...][None]).reshape(t.shape)

    @pl.when(j == 0)
    def _():
        q_ref[0] = head_norm(gq_ref)

    @pl.when(j == 1)
    def _():
        k_ref[0] = head_norm(gk_ref)

    @pl.when(j == 2)
    def _():
        v_ref[0] = t.astype(bf16)

    @pl.when(j == 3)
    def _():
        u_ref[0] = t


def _in_proj(x, g1, w_in_t, gq, gk, width, tm=512):
    B, L, D = x.shape
    assert w_in_t.shape == (4 * width, D) and L % tm == 0
    out_spec = pl.BlockSpec((1, width, tm), lambda b, i, j: (b, 0, i))
    return pl.pallas_call(
        _in_proj_kernel,
        grid=(B, L // tm, 4),
        in_specs=[
            pl.BlockSpec((1, tm, D), lambda b, i, j: (b, i, 0)),
            pl.BlockSpec((1, D), lambda b, i, j: (0, 0)),
            pl.BlockSpec((width, D), lambda b, i, j: (j, 0)),
            pl.BlockSpec((HEAD_DIM, 1), lambda b, i, j: (0, 0)),
            pl.BlockSpec((HEAD_DIM, 1), lambda b, i, j: (0, 0)),
        ],
        out_specs=[out_spec, out_spec, out_spec, out_spec],
        out_shape=[
            jax.ShapeDtypeStruct((B, width, L), f32),
            jax.ShapeDtypeStruct((B, width, L), f32),
            jax.ShapeDtypeStruct((B, width, L), bf16),
            jax.ShapeDtypeStruct((B, width, L), f32),
        ],
        scratch_shapes=[pltpu.VMEM((tm, D), bf16)],
        compiler_params=_params(("arbitrary", "arbitrary", "arbitrary")),
        name="in_proj",
    )(x, g1, w_in_t, gq, gk)


LOG2E = 1.4426950408889634
V_AUG_ROWS = HEAD_DIM + 16


def _moba_kernel(sp_ref, q_ref, k_ref, v_ref, o_ref, ktok_scr, vaug_scr):
    L = q_ref.shape[2]
    blk = MOBA_BLOCK
    nb = L // blk
    top_k = min(MOBA_TOPK, nb - 1)
    qscale = HEAD_DIM ** -0.5 * LOG2E
    pair = pl.program_id(1)

    r_i = lax.broadcasted_iota(jnp.int32, (blk, blk), 0)
    c_i = lax.broadcasted_iota(jnp.int32, (blk, blk), 1)
    causal = r_i <= c_i
    blk_id = lax.broadcasted_iota(jnp.int32, (nb, blk), 0)
    q_row = lax.broadcasted_iota(jnp.int32, (HEAD_DIM, blk), 0)

    ktok = k_ref[0].T
    kmean = jnp.concatenate(
        [jnp.mean(ktok[j * blk:(j + 1) * blk], axis=0, keepdims=True) for j in range(nb)], axis=0)
    lane = lax.broadcasted_iota(jnp.int32, ktok.shape, 1)
    key_off = jnp.bitwise_and(lax.broadcasted_iota(jnp.int32, ktok.shape, 0), blk - 1).astype(f32)
    ones_rows = jnp.where(lax.broadcasted_iota(jnp.int32, (V_AUG_ROWS - HEAD_DIM, L), 0) == 0, 1.0, 0.0).astype(bf16)
    for hh in range(2):
        kh = ktok if hh == 0 else pltpu.roll(ktok, HEAD_DIM, 1)
        ktok_scr[hh] = jnp.where(lane < HEAD_DIM, kh, jnp.where(lane < HEAD_DIM + 3, key_off, 0.0)).astype(bf16)
        vaug_scr[hh, 0:HEAD_DIM, :] = v_ref[0, hh * HEAD_DIM:(hh + 1) * HEAD_DIM, :]
        vaug_scr[hh, HEAD_DIM:, :] = ones_rows

    for hh in range(2):
        rows = slice(hh * HEAD_DIM, (hh + 1) * HEAD_DIM)
        head = pair * 2 + hh
        pieces = [jnp.full((HEAD_DIM, blk), sp_ref[head, t], f32) for t in range(3)]
        q_extra = jnp.where(q_row == 0, pieces[0], jnp.where(q_row == 1, pieces[1],
                            jnp.where(q_row == 2, pieces[2], 0.0))).astype(bf16)
        slope2 = sum(jnp.full((1, blk), sp_ref[head, t], f32) for t in range(3))

        def both_heads(a):
            z = jnp.zeros_like(a)
            return jnp.concatenate([a, z] if hh == 0 else [z, a], axis=0)

        for i in range(nb):
            cols = slice(i * blk, (i + 1) * blk)
            qf = q_ref[0, rows, cols]
            q_aug = jnp.concatenate([(qf * qscale).astype(bf16), q_extra], axis=0)
            s = [jnp.dot(ktok_scr[hh, j * blk:(j + 1) * blk, :], q_aug, preferred_element_type=f32)
                 for j in range(i + 1)]
            s[i] = jnp.where(causal, s[i], NEG)
            m = jnp.max(s[i], axis=0, keepdims=True)
            shifts = [None] * (i + 1)
            if i > 0:
                gate = jnp.dot(kmean, both_heads(qf), precision=HIGHEST, preferred_element_type=f32)
                rank = jnp.zeros((nb, blk), f32)
                for jp in range(i):
                    row = gate[jp:jp + 1, :]
                    beats = (row > gate) | ((row == gate) & (jp < blk_id))
                    rank = rank + jnp.where(beats, 1.0, 0.0)
                sel = jnp.where((blk_id < i) & (rank < top_k), 1.0, 0.0)
                offs = []
                for j in range(i):
                    selj = sel[j:j + 1, :] > 0.5
                    cj = slope2 * float((j - i) * blk)
                    m = jnp.maximum(m, jnp.where(selj, jnp.max(s[j], axis=0, keepdims=True) + cj, NEG))
                    offs.append((selj, cj))
                for j in range(i):
                    shifts[j] = jnp.where(offs[j][0], offs[j][1] - m, NEG)
            shifts[i] = -m
            acc = None
            for j in range(i + 1):
                p = jnp.exp2(s[j] + shifts[j]).astype(bf16)
                d = jnp.dot(vaug_scr[hh, :, j * blk:(j + 1) * blk], p, preferred_element_type=f32)
                acc = d if acc is None else acc + d
            o_ref[0, rows, cols] = acc[0:HEAD_DIM] / acc[HEAD_DIM:HEAD_DIM + 1]


def _moba(q_t, k_t, v_t):
    B, width, L = q_t.shape
    n_heads = width // HEAD_DIM
    assert n_heads % 2 == 0 and L % MOBA_BLOCK == 0 and MOBA_BLOCK & (MOBA_BLOCK - 1) == 0
    slope2 = jnp.exp2(-8.0 * (jnp.arange(n_heads, dtype=f32) + 1.0) / n_heads) * LOG2E
    hi = slope2.astype(bf16).astype(f32)
    mid = (slope2 - hi).astype(bf16).astype(f32)
    lo = (slope2 - hi - mid).astype(bf16).astype(f32)
    spec = pl.BlockSpec((1, 2 * HEAD_DIM, L), lambda b, h: (b, h, 0))
    return pl.pallas_call(
        _moba_kernel,
        grid=(B, n_heads // 2),
        in_specs=[pl.BlockSpec(memory_space=pltpu.SMEM), spec, spec, spec],
        out_specs=spec,
        out_shape=jax.ShapeDtypeStruct((B, width, L), f32),
        scratch_shapes=[
            pltpu.VMEM((2, L, 2 * HEAD_DIM), bf16),
            pltpu.VMEM((2, V_AUG_ROWS, L), bf16),
        ],
        compiler_params=_params(("arbitrary", "arbitrary")),
        name="moba",
    )(jnp.stack([hi, mid, lo], axis=1), q_t, k_t, v_t)


def _gelu_tanh(x):
    return 0.5 * x * (1.0 + jnp.tanh(math.sqrt(2.0 / math.pi) * (x + 0.044715 * (x * x * x))))


def _ssm_kernel(u_ref, lrr_ref, lir_ref, lrc_ref, lic_ref, ldt_ref, btr_ref, bti_ref,
                cr_ref, ci_ref, ctr_ref, cti_ref, dexp_ref, y_ref,
                lhs_scr, m_scr, ws_scr, wo_scr, kt_scr, sp_scr):
    B = u_ref.shape[0]
    L = u_ref.shape[2]
    T = SSM_CHUNK
    P = SSM_STATE
    Hc = SSM_GROUP
    nC = L // T

    for c in range(nC):
        for hp in range(Hc):
            lhs_scr[c * B:(c + 1) * B, hp * T:(hp + 1) * T] = u_ref[:, hp, c * T:(c + 1) * T].astype(bf16)

    dt = jnp.exp(ldt_ref[0])

    def lam_pow(lr, li, n):
        mag = jnp.exp(n * (lr * dt))
        ang = n * (li * dt)
        return mag * jnp.cos(ang), mag * jnp.sin(ang)

    lr_r, li_r = lrr_ref[0], lir_ref[0]
    lr_c, li_c = lrc_ref[0], lic_ref[0]

    ab_re, ab_im = lam_pow(lr_r, li_r, 1.0)
    den = lr_r * lr_r + li_r * li_r
    nr = ab_re - 1.0
    f_re = (nr * lr_r + ab_im * li_r) / den
    f_im = (ab_im * lr_r - nr * li_r) / den
    bb_re = f_re * btr_ref[0] - f_im * bti_ref[0]
    bb_im = f_re * bti_ref[0] + f_im * btr_ref[0]

    c_re, c_im = cr_ref[0], ci_ref[0]
    cb_re = (c_re[:, None, :] * bb_re[None, :, :] - c_im[:, None, :] * bb_im[None, :, :]).reshape(Hc * Hc, P)
    cb_im = (c_re[:, None, :] * bb_im[None, :, :] + c_im[:, None, :] * bb_re[None, :, :]).reshape(Hc * Hc, P)
    d_lane = lax.broadcasted_iota(jnp.int32, (P, T), 1).astype(f32)
    pd_re, pd_im = lam_pow(lr_c, li_c, d_lane)
    kt_scr[...] = (jnp.dot(cb_re, pd_re, precision=HIGHEST, preferred_element_type=f32)
                   - jnp.dot(cb_im, pd_im, precision=HIGHEST, preferred_element_type=f32))

    tau_i = lax.broadcasted_iota(jnp.int32, (T, T), 0)
    t_i = lax.broadcasted_iota(jnp.int32, (T, T), 1)
    causal = t_i >= tau_i

    def toeplitz_rows(hp, carry):
        for h in range(Hc):
            row = kt_scr[pl.ds(h * Hc + hp, 1), :]
            tile = pltpu.roll(jnp.broadcast_to(row, (T, T)), 0, 1, stride=1, stride_axis=0)
            m_scr[pl.ds(pl.multiple_of(hp * T, T), T), h * T:(h + 1) * T] = jnp.where(causal, tile, 0.0).astype(bf16)
        return carry

    lax.fori_loop(0, Hc, toeplitz_rows, 0)

    tau_s = lax.broadcasted_iota(jnp.int32, (T, P), 0).astype(f32)
    pe_re, pe_im = lam_pow(lr_r, li_r, (T - 1.0) - tau_s)
    for hp in range(Hc):
        br, bi = bb_re[hp:hp + 1, :], bb_im[hp:hp + 1, :]
        ws_scr[hp * T:(hp + 1) * T, :] = jnp.concatenate(
            [pe_re * br - pe_im * bi, pe_re * bi + pe_im * br], axis=1).astype(bf16)
    po_re, po_im = lam_pow(lr_c, li_c, d_lane + 1.0)
    ct_re, ct_im = ctr_ref[0], cti_ref[0]
    for h in range(Hc):
        cre, cim = ct_re[:, h:h + 1], ct_im[:, h:h + 1]
        wo_scr[:, h * T:(h + 1) * T] = jnp.concatenate(
            [cre * po_re - cim * po_im, -(cre * po_im + cim * po_re)], axis=0).astype(bf16)

    lhs = lhs_scr[...]
    y = jnp.dot(lhs, m_scr[...], preferred_element_type=f32)
    v = jnp.dot(lhs, ws_scr[...], preferred_element_type=f32)

    a_re, a_im = lam_pow(lr_r, li_r, float(T))
    a1 = jnp.concatenate([a_re, a_re], axis=1)
    a2 = jnp.concatenate([-a_im, a_im], axis=1)
    s = jnp.zeros((B, 2 * P), f32)
    for c in range(nC):
        sp_scr[c * B:(c + 1) * B, :] = s
        s = a1 * s + a2 * pltpu.roll(s, P, 1) + v[c * B:(c + 1) * B, :]

    y = y + jnp.dot(sp_scr[...].astype(bf16), wo_scr[...], preferred_element_type=f32)
    y = y + dexp_ref[0] * lhs.astype(f32)
    y = _gelu_tanh(y)
    for c in range(nC):
        for h in range(Hc):
            y_ref[:, h, c * T:(c + 1) * T] = y[c * B:(c + 1) * B, h * T:(h + 1) * T]


def _ssm(u_t, lambda_re, lambda_im, log_dt, b_re, b_im, c_re, c_im, d_skip):
    B, width, L = u_t.shape
    G, P = lambda_re.shape
    Hc, T = SSM_GROUP, SSM_CHUNK
    assert width == G * Hc and P == SSM_STATE and L % T == 0
    nC = L // T
    row = lambda a: a.reshape(G, 1, P)
    col = lambda a: a.reshape(G, P, 1)
    d_exp = jnp.repeat(d_skip.reshape(G, 1, Hc), T, axis=2)
    args = (u_t, row(lambda_re), row(lambda_im), col(lambda_re), col(lambda_im), log_dt.reshape(G, 1, 1),
            jnp.swapaxes(b_re, 1, 2), jnp.swapaxes(b_im, 1, 2), c_re, c_im,
            jnp.swapaxes(c_re, 1, 2), jnp.swapaxes(c_im, 1, 2), d_exp)
    g3 = lambda shp: pl.BlockSpec((1,) + shp, lambda g: (g, 0, 0))
    u_spec = pl.BlockSpec((B, Hc, L), lambda g: (0, g, 0))
    return pl.pallas_call(
        _ssm_kernel,
        grid=(G,),
        in_specs=[u_spec, g3((1, P)), g3((1, P)), g3((P, 1)), g3((P, 1)), g3((1, 1)),
                  g3((Hc, P)), g3((Hc, P)), g3((Hc, P)), g3((Hc, P)), g3((P, Hc)), g3((P, Hc)),
                  g3((1, Hc * T))],
        out_specs=u_spec,
        out_shape=jax.ShapeDtypeStruct((B, width, L), f32),
        scratch_shapes=[
            pltpu.VMEM((nC * B, Hc * T), bf16),
            pltpu.VMEM((Hc * T, Hc * T), bf16),
            pltpu.VMEM((Hc * T, 2 * P), bf16),
            pltpu.VMEM((2 * P, Hc * T), bf16),
            pltpu.VMEM((Hc * Hc, T), f32),
            pltpu.VMEM((nC * B, 2 * P), f32),
        ],
        compiler_params=_params(("arbitrary",)),
        name="ssm",
    )(*args)


def _mix_kernel(x_ref, a_ref, y_ref, wg_ref, bg_ref, ga_ref, gs_ref, wo_ref, o_ref):
    def rms_rows(t, g_ref):
        return t * lax.rsqrt(jnp.mean(t * t, axis=0, keepdims=True) + EPS) * g_ref[...]

    a_n = rms_rows(a_ref[0], ga_ref)
    yg = y_ref[0]
    z = jnp.dot(wg_ref[...], yg.astype(bf16), preferred_element_type=f32) + bg_ref[...]
    s_n = rms_rows(yg * jax.nn.sigmoid(z), gs_ref)
    mixed_t = jnp.concatenate([a_n, s_n], axis=0).astype(bf16)
    delta = lax.dot_general(mixed_t, wo_ref[...], (((0,), (0,)), ((), ())), preferred_element_type=f32)
    o_ref[0] = x_ref[0] + delta


def _mix(x, attn_t, yg_t, w_glu_t, b_glu, g_attn, g_ssm, w_out, tn=256):
    B, L, D = x.shape
    aw, sw = attn_t.shape[1], yg_t.shape[1]
    assert aw + sw == D and L % tn == 0
    const = lambda shp: pl.BlockSpec(shp, lambda b, i: (0, 0))
    return pl.pallas_call(
        _mix_kernel,
        grid=(B, L // tn),
        in_specs=[
            pl.BlockSpec((1, tn, D), lambda b, i: (b, i, 0)),
            pl.BlockSpec((1, aw, tn), lambda b, i: (b, 0, i)),
            pl.BlockSpec((1, sw, tn), lambda b, i: (b, 0, i)),
            const((sw, sw)), const((sw, 1)), const((aw, 1)), const((sw, 1)), const((D, D)),
        ],
        out_specs=pl.BlockSpec((1, tn, D), lambda b, i: (b, i, 0)),
        out_shape=jax.ShapeDtypeStruct((B, L, D), f32),
        compiler_params=_params(("arbitrary", "arbitrary")),
        name="mix",
    )(x, attn_t, yg_t, w_glu_t, b_glu, g_attn, g_ssm, w_out)


def _ffn_kernel(x_ref, g2_ref, w1_ref, w2_ref, o_ref, h_scr):
    f = pl.program_id(1)

    @pl.when(f == 0)
    def _():
        x = x_ref[...]
        ms = jnp.mean(x * x, axis=-1, keepdims=True)
        h_scr[...] = (x * lax.rsqrt(ms + EPS) * g2_ref[...]).astype(bf16)
        o_ref[...] = x

    a = jnp.dot(h_scr[...], w1_ref[...], preferred_element_type=f32)
    a = jnp.square(jnp.maximum(a, 0.0)).astype(bf16)
    o_ref[...] += jnp.dot(a, w2_ref[...], preferred_element_type=f32)


def _ffn(x2d, g2, w1, w2, tm=512, tf=1024):
    N, D = x2d.shape
    F = w1.shape[1]
    assert N % tm == 0 and F % tf == 0
    return pl.pallas_call(
        _ffn_kernel,
        grid=(N // tm, F // tf),
        in_specs=[
            pl.BlockSpec((tm, D), lambda i, f: (i, 0)),
            pl.BlockSpec((1, D), lambda i, f: (0, 0)),
            pl.BlockSpec((D, tf), lambda i, f: (0, f)),
            pl.BlockSpec((tf, D), lambda i, f: (f, 0)),
        ],
        out_specs=pl.BlockSpec((tm, D), lambda i, f: (i, 0)),
        out_shape=jax.ShapeDtypeStruct((N, D), f32),
        scratch_shapes=[pltpu.VMEM((tm, D), bf16)],
        compiler_params=_params(("arbitrary", "arbitrary")),
        name="ffn",
    )(x2d, g2, w1, w2)


def _layer(x, norm1_gain, w_in, q_norm_gain, k_norm_gain, attn_out_gain, lambda_re, lambda_im, log_dt,
           b_re, b_im, c_re, c_im, d_skip, w_glu, b_glu, ssm_out_gain, w_out, norm2_gain, w_ff1, w_ff2):
    B, L, D = x.shape
    sw = d_skip.shape[0]
    aw = D - sw
    assert aw == sw and w_in.shape == (D, 3 * aw + sw)
    q_t, k_t, v_t, u_t = _in_proj(
        x, norm1_gain.reshape(1, D), w_in.T.astype(bf16),
        q_norm_gain.reshape(HEAD_DIM, 1), k_norm_gain.reshape(HEAD_DIM, 1), aw)
    attn_t = _moba(q_t, k_t, v_t)
    yg_t = _ssm(u_t, lambda_re, lambda_im, log_dt, b_re, b_im, c_re, c_im, d_skip)
    x1 = _mix(x, attn_t, yg_t, w_glu.T.astype(bf16), b_glu.reshape(sw, 1),
              attn_out_gain.reshape(aw, 1), ssm_out_gain.reshape(sw, 1), w_out.astype(bf16))
    out = _ffn(x1.reshape(B * L, D), norm2_gain.reshape(1, D), w_ff1.astype(bf16), w_ff2.astype(bf16))
    return out.reshape(B, L, D)


def kernel(x, norm1_gain, w_in, q_norm_gain, k_norm_gain, attn_out_gain, lambda_re, lambda_im, log_dt, b_re, b_im, c_re, c_im, d_skip, w_glu, b_glu, ssm_out_gain, w_out, norm2_gain, w_ff1, w_ff2):
    per_layer = (norm1_gain, w_in, q_norm_gain, k_norm_gain, attn_out_gain, lambda_re, lambda_im, log_dt,
                 b_re, b_im, c_re, c_im, d_skip, w_glu, b_glu, ssm_out_gain, w_out, norm2_gain, w_ff1, w_ff2)
    for i in range(w_in.shape[0]):
        x = _layer(x, *(p[i] for p in per_layer))
    return x
```

```python
import functools
import math

import jax
import jax.numpy as jnp
from jax import lax
from jax.experimental import pallas as pl
from jax.experimental.pallas import tpu as pltpu

f32 = jnp.float32
bf16 = jnp.bfloat16

EPS = 1e-6
NEG = -1e30
HEAD_DIM = 64
MOBA_BLOCK = 256
MOBA_TOPK = 3
SSM_GROUP = 16
SSM_STATE = 64
SSM_CHUNK = 128
VMEM_LIMIT = 56 * 1024 * 1024
HIGHEST = lax.Precision.HIGHEST


def _params(sem):
    return pltpu.CompilerParams(dimension_semantics=sem, vmem_limit_bytes=VMEM_LIMIT)


IN_PROJ_CHUNK = 256


def _in_proj_kernel(x_ref, g1_ref, w_ref, gq_ref, gk_ref, q_ref, k_ref, v_ref, u_ref, h_scr):
    j = pl.program_id(2)
    tm = x_ref.shape[1]

    @pl.when(j == 0)
    def _():
        x = x_ref[0]
        ms = jnp.mean(x * x, axis=-1, keepdims=True)
        h_scr[...] = (x * lax.rsqrt(ms + EPS) * g1_ref[...]).astype(bf16)

    def project(out_ref, finish):
        for c in range(0, w_ref.shape[0], IN_PROJ_CHUNK):
            rows = slice(c, c + IN_PROJ_CHUNK)
            t = lax.dot_general(w_ref[rows, :], h_scr[...], (((1,), (1,)), ((), ())), preferred_element_type=f32)
            out_ref[0, rows, :] = finish(t)

    def head_norm(g_ref):
        def finish(t):
            t3 = t.reshape(t.shape[0] // HEAD_DIM, HEAD_DIM, tm)
            ms = jnp.mean(t3 * t3, axis=1, keepdims=True)
            return (t3 * lax.rsqrt(ms + EPS) * g_ref[...][None]).reshape(t.shape)
        return finish

    @pl.when(j == 0)
    def _():
        project(q_ref, head_norm(gq_ref))

    @pl.when(j == 1)
    def _():
        project(k_ref, head_norm(gk_ref))

    @pl.when(j == 2)
    def _():
        project(v_ref, lambda t: t.astype(bf16))

    @pl.when(j == 3)
    def _():
        project(u_ref, lambda t: t)


def _in_proj(x, g1, w_in_t, gq, gk, width, tm=512):
    B, L, D = x.shape
    assert w_in_t.shape == (4 * width, D) and L % tm == 0
    out_spec = pl.BlockSpec((1, width, tm), lambda b, i, j: (b, 0, i))
    return pl.pallas_call(
        _in_proj_kernel,
        grid=(B, L // tm, 4),
        in_specs=[
            pl.BlockSpec((1, tm, D), lambda b, i, j: (b, i, 0)),
            pl.BlockSpec((1, D), lambda b, i, j: (0, 0)),
            pl.BlockSpec((width, D), lambda b, i, j: (j, 0)),
            pl.BlockSpec((HEAD_DIM, 1), lambda b, i, j: (0, 0)),
            pl.BlockSpec((HEAD_DIM, 1), lambda b, i, j: (0, 0)),
        ],
        out_specs=[out_spec, out_spec, out_spec, out_spec],
        out_shape=[
            jax.ShapeDtypeStruct((B, width, L), f32),
            jax.ShapeDtypeStruct((B, width, L), f32),
            jax.ShapeDtypeStruct((B, width, L), bf16),
            jax.ShapeDtypeStruct((B, width, L), f32),
        ],
        scratch_shapes=[pltpu.VMEM((tm, D), bf16)],
        compiler_params=_params(("arbitrary", "arbitrary", "arbitrary")),
        name="in_proj",
    )(x, g1, w_in_t, gq, gk)


LOG2E = 1.4426950408889634
V_AUG_ROWS = HEAD_DIM + 16
MOBA_PIPE_DEPTH = 5


def _moba_kernel(sp_ref, q_ref, k_ref, v_ref, o_ref, ktok_scr, vaug_scr, s_scr, *, single_pass):
    L = q_ref.shape[2]
    blk = MOBA_BLOCK
    nb = L // blk
    top_k = min(MOBA_TOPK, nb - 1)
    qscale = HEAD_DIM ** -0.5 * LOG2E
    pair = pl.program_id(1)

    r_i = lax.broadcasted_iota(jnp.int32, (blk, blk), 0)
    c_i = lax.broadcasted_iota(jnp.int32, (blk, blk), 1)
    causal = r_i <= c_i
    blk_id = lax.broadcasted_iota(jnp.int32, (nb, blk), 0)
    q_row = lax.broadcasted_iota(jnp.int32, (HEAD_DIM, blk), 0)
    q_off = lax.broadcasted_iota(jnp.int32, (1, blk), 1).astype(f32)

    k_t = k_ref[0]
    ktok = k_t.T
    kmean = jnp.concatenate(
        [jnp.mean(ktok[j * blk:(j + 1) * blk], axis=0, keepdims=True) for j in range(nb)], axis=0)
    lane = lax.broadcasted_iota(jnp.int32, ktok.shape, 1)
    key_off = jnp.bitwise_and(lax.broadcasted_iota(jnp.int32, ktok.shape, 0), blk - 1).astype(f32)
    ones_rows = jnp.where(lax.broadcasted_iota(jnp.int32, (V_AUG_ROWS - HEAD_DIM, L), 0) == 0, 1.0, 0.0).astype(bf16)
    for hh in range(2):
        kh = ktok if hh == 0 else pltpu.roll(ktok, HEAD_DIM, 1)
        ktok_scr[hh] = jnp.where(lane < HEAD_DIM, kh, jnp.where(lane < HEAD_DIM + 3, key_off, 0.0)).astype(bf16)
        vaug_scr[hh, 0:HEAD_DIM, :] = v_ref[0, hh * HEAD_DIM:(hh + 1) * HEAD_DIM, :]
        vaug_scr[hh, HEAD_DIM:, :] = ones_rows

    steps = []
    for hh in range(2):
        rows = slice(hh * HEAD_DIM, (hh + 1) * HEAD_DIM)
        head = pair * 2 + hh
        pieces = [jnp.full((HEAD_DIM, blk), sp_ref[head, t], f32) for t in range(3)]
        q_extra = jnp.where(q_row == 0, pieces[0], jnp.where(q_row == 1, pieces[1],
                            jnp.where(q_row == 2, pieces[2], 0.0))).astype(bf16)
        slope2 = sum(jnp.full((1, blk), sp_ref[head, t], f32) for t in range(3))

        if single_pass:
            kh_t = k_t[rows, :]
            kn2 = jnp.sum(kh_t * kh_t, axis=0, keepdims=True)
            kmax = []
            for j in range(nb):
                bj = jnp.sqrt(jnp.max(kn2[:, j * blk:(j + 1) * blk], axis=1, keepdims=True))
                kmax.append(bj if j == 0 else jnp.maximum(kmax[-1], bj))

        def both_heads(a):
            z = jnp.zeros_like(a)
            return jnp.concatenate([a, z] if hh == 0 else [z, a], axis=0)

        for i in range(nb):
            cols = slice(i * blk, (i + 1) * blk)
            qf = q_ref[0, rows, cols]
            q_aug = jnp.concatenate([(qf * qscale).astype(bf16), q_extra], axis=0)

            def scores(j, q_aug=q_aug, i=i, hh=hh):
                s = jnp.dot(ktok_scr[hh, j * blk:(j + 1) * blk, :], q_aug, preferred_element_type=f32)
                return jnp.where(causal, s, NEG) if j == i else s

            sel_off = []
            if i > 0:
                gate = jnp.dot(kmean, both_heads(qf), precision=HIGHEST, preferred_element_type=f32)
                rank = jnp.zeros((nb, blk), f32)
                for jp in range(i):
                    row = gate[jp:jp + 1, :]
                    beats = (row > gate) | ((row == gate) & (jp < blk_id))
                    rank = rank + jnp.where(beats, 1.0, 0.0)
                sel = jnp.where((blk_id < i) & (rank < top_k), 1.0, 0.0)
                sel_off = [(sel[j:j + 1, :] > 0.5, slope2 * float((j - i) * blk)) for j in range(i)]

            if single_pass:
                q_norm = jnp.sqrt(jnp.sum(qf * qf, axis=0, keepdims=True))
                m = (qscale * q_norm) * kmax[i] + slope2 * q_off
                shifts = [jnp.where(selj, cj - m, NEG) for selj, cj in sel_off] + [-m]
                for j in range(i + 1):
                    steps.append((scores, j, shifts[j], hh, rows, cols, j == 0, j == i))
                continue
            s = [scores(j) for j in range(i + 1)]
            m = jnp.max(s[i], axis=0, keepdims=True)
            for j in range(i):
                selj, cj = sel_off[j]
                m = jnp.maximum(m, jnp.where(selj, jnp.max(s[j], axis=0, keepdims=True) + cj, NEG))
            shifts = [jnp.where(selj, cj - m, NEG) for selj, cj in sel_off] + [-m]
            acc = None
            for j in range(i + 1):
                p = jnp.exp2(s[j] + shifts[j]).astype(bf16)
                d = jnp.dot(vaug_scr[hh, :, j * blk:(j + 1) * blk], p, preferred_element_type=f32)
                acc = d if acc is None else acc + d
            o_ref[0, rows, cols] = acc[0:HEAD_DIM] / acc[HEAD_DIM:HEAD_DIM + 1]

    if single_pass:
        depth = MOBA_PIPE_DEPTH

        def stage1(n):
            scores, j = steps[n][0], steps[n][1]
            s_scr[n % (depth + 1)] = scores(j)

        for n in range(depth):
            stage1(n)
        acc = None
        for n, (_, j, shift, hh, rows, cols, first, last) in enumerate(steps):
            if n + depth < len(steps):
                stage1(n + depth)
            p = jnp.exp2(s_scr[n % (depth + 1)] + shift).astype(bf16)
            d = jnp.dot(vaug_scr[hh, :, j * blk:(j + 1) * blk], p, preferred_element_type=f32)
            acc = d if first else acc + d
            if last:
                o_ref[0, rows, cols] = acc[0:HEAD_DIM] / acc[HEAD_DIM:HEAD_DIM + 1]


SINGLE_PASS_MAX_GAP = 64.0


def _moba(q_t, k_t, v_t, gq, gk):
    B, width, L = q_t.shape
    n_heads = width // HEAD_DIM
    assert n_heads % 2 == 0 and L % MOBA_BLOCK == 0 and MOBA_BLOCK & (MOBA_BLOCK - 1) == 0
    slope2 = jnp.exp2(-8.0 * (jnp.arange(n_heads, dtype=f32) + 1.0) / n_heads) * LOG2E
    hi = slope2.astype(bf16).astype(f32)
    mid = (slope2 - hi).astype(bf16).astype(f32)
    lo = (slope2 - hi - mid).astype(bf16).astype(f32)
    pieces = jnp.stack([hi, mid, lo], axis=1)
    spec = pl.BlockSpec((1, 2 * HEAD_DIM, L), lambda b, h: (b, h, 0))

    def call(single_pass):
        return pl.pallas_call(
            functools.partial(_moba_kernel, single_pass=single_pass),
            grid=(B, n_heads // 2),
            in_specs=[pl.BlockSpec(memory_space=pltpu.SMEM), spec, spec, spec],
            out_specs=spec,
            out_shape=jax.ShapeDtypeStruct((B, width, L), f32),
            scratch_shapes=[
                pltpu.VMEM((2, L, 2 * HEAD_DIM), bf16),
                pltpu.VMEM((2, V_AUG_ROWS, L), bf16),
                pltpu.VMEM((MOBA_PIPE_DEPTH + 1, MOBA_BLOCK, MOBA_BLOCK), f32),
            ],
            compiler_params=_params(("arbitrary", "arbitrary")),
            name="moba_single_pass" if single_pass else "moba_two_pass",
        )(pieces, q_t, k_t, v_t)

    gap = 2.0 * (HEAD_DIM ** -0.5 * LOG2E) * HEAD_DIM * jnp.max(jnp.abs(gq)) * jnp.max(jnp.abs(gk))
    return lax.cond(gap <= SINGLE_PASS_MAX_GAP, lambda: call(True), lambda: call(False))


SSM_ROW_PITCH = 24


def _gelu_tanh(x):
    c = math.sqrt(2.0 / math.pi)
    w = x * ((-2.0 * c * 0.044715 * LOG2E) * (x * x) + (-2.0 * c * LOG2E))
    return x / (1.0 + jnp.exp2(w))


def _ssm_kernel(u_ref, lrr_ref, lir_ref, lrc_ref, lic_ref, ldt_ref, btr_ref, bti_ref,
                cr_ref, ci_ref, ctr_ref, cti_ref, dexp_ref, y_ref,
                lhs_scr, m_scr, ws_scr, wo_scr, sp_scr, flat_scr):
    B = u_ref.shape[0]
    L = u_ref.shape[2]
    T = SSM_CHUNK
    P = SSM_STATE
    Hc = SSM_GROUP
    nC = L // T
    pitch = SSM_ROW_PITCH

    for c in range(nC):
        for b in range(B):
            flat_scr[(c * B + b) * pitch:(c * B + b) * pitch + Hc, :] = u_ref[b, :, c * T:(c + 1) * T]
    for c in range(nC):
        for hp in range(Hc):
            lhs_scr[c * B:(c + 1) * B, hp * T:(hp + 1) * T] = flat_scr[pl.ds(c * B * pitch + hp, B, stride=pitch), :].astype(bf16)

    dt = jnp.exp(ldt_ref[0])

    def lam_pow(lr, li, n):
        mag = jnp.exp(n * (lr * dt))
        ang = n * (li * dt)
        return mag * jnp.cos(ang), mag * jnp.sin(ang)

    lr_r, li_r = lrr_ref[0], lir_ref[0]
    lr_c, li_c = lrc_ref[0], lic_ref[0]

    ab_re, ab_im = lam_pow(lr_r, li_r, 1.0)
    den = lr_r * lr_r + li_r * li_r
    nr = ab_re - 1.0
    f_re = (nr * lr_r + ab_im * li_r) / den
    f_im = (ab_im * lr_r - nr * li_r) / den
    bb_re = f_re * btr_ref[0] - f_im * bti_ref[0]
    bb_im = f_re * bti_ref[0] + f_im * btr_ref[0]

    tau_s = lax.broadcasted_iota(jnp.int32, (T, P), 0).astype(f32)
    pe_re, pe_im = lam_pow(lr_r, li_r, (T - 1.0) - tau_s)
    for hp in range(Hc):
        br, bi = bb_re[hp:hp + 1, :], bb_im[hp:hp + 1, :]
        ws_scr[hp * T:(hp + 1) * T, :] = jnp.concatenate(
            [pe_re * br - pe_im * bi, pe_re * bi + pe_im * br], axis=1).astype(bf16)
    d_lane = lax.broadcasted_iota(jnp.int32, (P, T), 1).astype(f32)
    po_re, po_im = lam_pow(lr_c, li_c, d_lane + 1.0)
    ct_re, ct_im = ctr_ref[0], cti_ref[0]
    for h in range(Hc):
        cre, cim = ct_re[:, h:h + 1], ct_im[:, h:h + 1]
        wo_scr[:, h * T:(h + 1) * T] = jnp.concatenate(
            [cre * po_re - cim * po_im, -(cre * po_im + cim * po_re)], axis=0).astype(bf16)

    v = jnp.dot(lhs_scr[...], ws_scr[...], preferred_element_type=f32)
    a_re, a_im = lam_pow(lr_r, li_r, float(T))
    a1 = jnp.concatenate([a_re, a_re], axis=1)
    a2 = jnp.concatenate([-a_im, a_im], axis=1)
    s = jnp.zeros((B, 2 * P), f32)
    for c in range(nC):
        sp_scr[c * B:(c + 1) * B, :] = s.astype(bf16)
        s = a1 * s + a2 * pltpu.roll(s, P, 1) + v[c * B:(c + 1) * B, :]

    c_re, c_im = cr_ref[0], ci_ref[0]
    cb_re = (c_re[:, None, :] * bb_re[None, :, :] - c_im[:, None, :] * bb_im[None, :, :]).reshape(Hc * Hc, P)
    cb_im = (c_re[:, None, :] * bb_im[None, :, :] + c_im[:, None, :] * bb_re[None, :, :]).reshape(Hc * Hc, P)
    pd_re, pd_im = lam_pow(lr_c, li_c, d_lane)
    kt = (jnp.dot(cb_re, pd_re, precision=HIGHEST, preferred_element_type=f32)
          - jnp.dot(cb_im, pd_im, precision=HIGHEST, preferred_element_type=f32))

    tau_i = lax.broadcasted_iota(jnp.int32, (T, T), 0)
    t_i = lax.broadcasted_iota(jnp.int32, (T, T), 1)
    causal = t_i >= tau_i

    for h0 in range(0, Hc, 2):
        cols = slice(h0 * T, (h0 + 2) * T)
        for h in (h0, h0 + 1):
            for hp in range(Hc):
                row = kt[h * Hc + hp:h * Hc + hp + 1, :]
                tile = pltpu.roll(jnp.broadcast_to(row, (T, T)), 0, 1, stride=1, stride_axis=0)
                m_scr[hp * T:(hp + 1) * T, h * T:(h + 1) * T] = jnp.where(causal, tile, 0.0).astype(bf16)
        lhs2 = lhs_scr[:, cols]
        y = jnp.dot(lhs_scr[...], m_scr[:, cols], preferred_element_type=f32)
        y = y + jnp.dot(sp_scr[...], wo_scr[:, cols], preferred_element_type=f32)
        y = _gelu_tanh(y + dexp_ref[0, :, cols] * lhs2.astype(f32))
        for c in range(nC):
            for k, h in enumerate((h0, h0 + 1)):
                flat_scr[pl.ds(c * B * pitch + h, B, stride=pitch), :] = y[c * B:(c + 1) * B, k * T:(k + 1) * T]
    for c in range(nC):
        for b in range(B):
            y_ref[b, :, c * T:(c + 1) * T] = flat_scr[(c * B + b) * pitch:(c * B + b) * pitch + Hc, :]


def _ssm(u_t, lambda_re, lambda_im, log_dt, b_re, b_im, c_re, c_im, d_skip):
    B, width, L = u_t.shape
    G, P = lambda_re.shape
    Hc, T = SSM_GROUP, SSM_CHUNK
    assert width == G * Hc and P == SSM_STATE and L % T == 0
    nC = L // T
    row = lambda a: a.reshape(G, 1, P)
    col = lambda a: a.reshape(G, P, 1)
    d_exp = jnp.repeat(d_skip.reshape(G, 1, Hc), T, axis=2)
    args = (u_t, row(lambda_re), row(lambda_im), col(lambda_re), col(lambda_im), log_dt.reshape(G, 1, 1),
            jnp.swapaxes(b_re, 1, 2), jnp.swapaxes(b_im, 1, 2), c_re, c_im,
            jnp.swapaxes(c_re, 1, 2), jnp.swapaxes(c_im, 1, 2), d_exp)
    g3 = lambda shp: pl.BlockSpec((1,) + shp, lambda g: (g, 0, 0))
    u_spec = pl.BlockSpec((B, Hc, L), lambda g: (0, g, 0))
    return pl.pallas_call(
        _ssm_kernel,
        grid=(G,),
        in_specs=[u_spec, g3((1, P)), g3((1, P)), g3((P, 1)), g3((P, 1)), g3((1, 1)),
                  g3((Hc, P)), g3((Hc, P)), g3((Hc, P)), g3((Hc, P)), g3((P, Hc)), g3((P, Hc)),
                  g3((1, Hc * T))],
        out_specs=u_spec,
        out_shape=jax.ShapeDtypeStruct((B, width, L), f32),
        scratch_shapes=[
            pltpu.VMEM((nC * B, Hc * T), bf16),
            pltpu.VMEM((Hc * T, Hc * T), bf16),
            pltpu.VMEM((Hc * T, 2 * P), bf16),
            pltpu.VMEM((2 * P, Hc * T), bf16),
            pltpu.VMEM((nC * B, 2 * P), bf16),
            pltpu.VMEM((nC * B * SSM_ROW_PITCH, T), f32),
        ],
        compiler_params=_params(("arbitrary",)),
        name="ssm",
    )(*args)


MIX_CHUNK = 256


def _mix_kernel(x_ref, a_ref, y_ref, wg_ref, bg_ref, ga_ref, gs_ref, wo_ref, o_ref):
    def rms_rows(t, g_ref):
        return t * lax.rsqrt(jnp.mean(t * t, axis=0, keepdims=True) + EPS) * g_ref[...]

    for c in range(0, x_ref.shape[1], MIX_CHUNK):
        tok = slice(c, c + MIX_CHUNK)
        a_n = rms_rows(a_ref[0, :, tok], ga_ref)
        yg = y_ref[0, :, tok]
        z = jnp.dot(wg_ref[...], yg.astype(bf16), preferred_element_type=f32) + bg_ref[...]
        s_n = rms_rows(yg * jax.nn.sigmoid(z), gs_ref)
        mixed_t = jnp.concatenate([a_n, s_n], axis=0).astype(bf16)
        delta = lax.dot_general(mixed_t, wo_ref[...], (((0,), (0,)), ((), ())), preferred_element_type=f32)
        o_ref[0, tok, :] = x_ref[0, tok, :] + delta


def _mix(x, attn_t, yg_t, w_glu_t, b_glu, g_attn, g_ssm, w_out, tn=512):
    B, L, D = x.shape
    aw, sw = attn_t.shape[1], yg_t.shape[1]
    assert aw + sw == D and L % tn == 0 and tn % MIX_CHUNK == 0
    const = lambda shp: pl.BlockSpec(shp, lambda b, i: (0, 0), pipeline_mode=pl.Buffered(1))
    return pl.pallas_call(
        _mix_kernel,
        grid=(B, L // tn),
        in_specs=[
            pl.BlockSpec((1, tn, D), lambda b, i: (b, i, 0)),
            pl.BlockSpec((1, aw, tn), lambda b, i: (b, 0, i)),
            pl.BlockSpec((1, sw, tn), lambda b, i: (b, 0, i)),
            const((sw, sw)), const((sw, 1)), const((aw, 1)), const((sw, 1)), const((D, D)),
        ],
        out_specs=pl.BlockSpec((1, tn, D), lambda b, i: (b, i, 0)),
        out_shape=jax.ShapeDtypeStruct((B, L, D), f32),
        compiler_params=_params(("arbitrary", "arbitrary")),
        name="mix",
    )(x, attn_t, yg_t, w_glu_t, b_glu, g_attn, g_ssm, w_out)


def _ffn_kernel(x_ref, g2_ref, w1_ref, w2_ref, o_ref, h_scr):
    f = pl.program_id(1)

    @pl.when(f == 0)
    def _():
        x = x_ref[...]
        ms = jnp.mean(x * x, axis=-1, keepdims=True)
        h_scr[...] = (x * lax.rsqrt(ms + EPS) * g2_ref[...]).astype(bf16)
        o_ref[...] = x

    a = jnp.dot(h_scr[...], w1_ref[...], preferred_element_type=f32)
    a = jnp.square(jnp.maximum(a, 0.0)).astype(bf16)
    o_ref[...] += jnp.dot(a, w2_ref[...], preferred_element_type=f32)


def _ffn(x2d, g2, w1, w2, tm=512, tf=1024):
    N, D = x2d.shape
    F = w1.shape[1]
    assert N % tm == 0 and F % tf == 0
    return pl.pallas_call(
        _ffn_kernel,
        grid=(N // tm, F // tf),
        in_specs=[
            pl.BlockSpec((tm, D), lambda i, f: (i, 0)),
            pl.BlockSpec((1, D), lambda i, f: (0, 0)),
            pl.BlockSpec((D, tf), lambda i, f: (0, f)),
            pl.BlockSpec((tf, D), lambda i, f: (f, 0)),
        ],
        out_specs=pl.BlockSpec((tm, D), lambda i, f: (i, 0)),
        out_shape=jax.ShapeDtypeStruct((N, D), f32),
        scratch_shapes=[pltpu.VMEM((tm, D), bf16)],
        compiler_params=_params(("arbitrary", "arbitrary")),
        name="ffn",
    )(x2d, g2, w1, w2)


def _layer(x, norm1_gain, w_in, q_norm_gain, k_norm_gain, attn_out_gain, lambda_re, lambda_im, log_dt,
           b_re, b_im, c_re, c_im, d_skip, w_glu, b_glu, ssm_out_gain, w_out, norm2_gain, w_ff1, w_ff2):
    B, L, D = x.shape
    sw = d_skip.shape[0]
    aw = D - sw
    assert aw == sw and w_in.shape == (D, 3 * aw + sw)
    q_t, k_t, v_t, u_t = _in_proj(
        x, norm1_gain.reshape(1, D), w_in.T.astype(bf16),
        q_norm_gain.reshape(HEAD_DIM, 1), k_norm_gain.reshape(HEAD_DIM, 1), aw)
    attn_t = _moba(q_t, k_t, v_t, q_norm_gain, k_norm_gain)
    yg_t = _ssm(u_t, lambda_re, lambda_im, log_dt, b_re, b_im, c_re, c_im, d_skip)
    x1 = _mix(x, attn_t, yg_t, w_glu.T.astype(bf16), b_glu.reshape(sw, 1),
              attn_out_gain.reshape(aw, 1), ssm_out_gain.reshape(sw, 1), w_out.astype(bf16))
    out = _ffn(x1.reshape(B * L, D), norm2_gain.reshape(1, D), w_ff1.astype(bf16), w_ff2.astype(bf16))
    return out.reshape(B, L, D)


def kernel(x, norm1_gain, w_in, q_norm_gain, k_norm_gain, attn_out_gain, lambda_re, lambda_im, log_dt, b_re, b_im, c_re, c_im, d_skip, w_glu, b_glu, ssm_out_gain, w_out, norm2_gain, w_ff1, w_ff2):
    per_layer = (norm1_gain, w_in, q_norm_gain, k_norm_gain, attn_out_gain, lambda_re, lambda_im, log_dt,
                 b_re, b_im, c_re, c_im, d_skip, w_glu, b_glu, ssm_out_gain, w_out, norm2_gain, w_ff1, w_ff2)
    for i in range(w_in.shape[0]):
        x = _layer(x, *(p[i] for p in per_layer))
    return x
```

```python
import functools
import math

import jax
import jax.numpy as jnp
from jax import lax
from jax.experimental import pallas as pl
from jax.experimental.pallas import tpu as pltpu

f32 = jnp.float32
bf16 = jnp.bfloat16

EPS = 1e-6
NEG = -1e30
HEAD_DIM = 64
MOBA_BLOCK = 256
MOBA_TOPK = 3
SSM_GROUP = 16
SSM_STATE = 64
SSM_CHUNK = 128
VMEM_LIMIT = 56 * 1024 * 1024
HIGHEST = lax.Precision.HIGHEST


def _params(sem):
    return pltpu.CompilerParams(dimension_semantics=sem, vmem_limit_bytes=VMEM_LIMIT)


IN_PROJ_CHUNK = 256


def _in_proj_kernel(x_ref, g1_ref, w_ref, gq_ref, gk_ref, q_ref, k_ref, v_ref, u_ref, h_scr):
    j = pl.program_id(1)
    i = pl.program_id(2)
    tm = x_ref.shape[1]

    @pl.when(j == 0)
    def _():
        x = x_ref[0]
        ms = jnp.mean(x * x, axis=-1, keepdims=True)
        h_scr[i] = (x * lax.rsqrt(ms + EPS) * g1_ref[...]).astype(bf16)

    def project(out_ref, finish):
        for c in range(0, w_ref.shape[1], IN_PROJ_CHUNK):
            rows = slice(c, c + IN_PROJ_CHUNK)
            t = lax.dot_general(w_ref[:, rows], h_scr[i], (((0,), (1,)), ((), ())), preferred_element_type=f32)
            out_ref[0, rows, :] = finish(t)

    def head_norm(g_ref):
        def finish(t):
            t3 = t.reshape(t.shape[0] // HEAD_DIM, HEAD_DIM, tm)
            ms = jnp.mean(t3 * t3, axis=1, keepdims=True)
            return (t3 * lax.rsqrt(ms + EPS) * g_ref[...][None]).reshape(t.shape)
        return finish

    @pl.when(j == 0)
    def _():
        project(q_ref, head_norm(gq_ref))

    @pl.when(j == 1)
    def _():
        project(k_ref, head_norm(gk_ref))

    @pl.when(j == 2)
    def _():
        project(v_ref, lambda t: t.astype(bf16))

    @pl.when(j == 3)
    def _():
        project(u_ref, lambda t: t)


def _in_proj(x, g1, w_in, gq, gk, width, tm=512):
    B, L, D = x.shape
    assert w_in.shape == (D, 4 * width) and L % tm == 0
    n_i = L // tm
    last = n_i - 1

    def out_spec(jj):
        return pl.BlockSpec((1, width, tm),
                            lambda b, j, i: (b, 0, jnp.where(j == jj, i, jnp.where(j < jj, 0, last))))

    return pl.pallas_call(
        _in_proj_kernel,
        grid=(B, 4, n_i),
        in_specs=[
            pl.BlockSpec((1, tm, D), lambda b, j, i: (b, jnp.where(j == 0, i, last), 0)),
            pl.BlockSpec((1, D), lambda b, j, i: (0, 0)),
            pl.BlockSpec((D, width), lambda b, j, i: (0, j)),
            pl.BlockSpec((HEAD_DIM, 1), lambda b, j, i: (0, 0)),
            pl.BlockSpec((HEAD_DIM, 1), lambda b, j, i: (0, 0)),
        ],
        out_specs=[out_spec(0), out_spec(1), out_spec(2), out_spec(3)],
        out_shape=[
            jax.ShapeDtypeStruct((B, width, L), f32),
            jax.ShapeDtypeStruct((B, width, L), f32),
            jax.ShapeDtypeStruct((B, width, L), bf16),
            jax.ShapeDtypeStruct((B, width, L), f32),
        ],
        scratch_shapes=[pltpu.VMEM((n_i, tm, D), bf16)],
        compiler_params=_params(("arbitrary", "arbitrary", "arbitrary")),
        name="in_proj",
    )(x, g1, w_in, gq, gk)


LOG2E = 1.4426950408889634
V_AUG_ROWS = HEAD_DIM + 16
MOBA_PIPE_DEPTH = 5


def _moba_kernel(sp_ref, q_ref, k_ref, v_ref, o_ref, ktok_scr, vaug_scr, s_scr, *, single_pass):
    L = q_ref.shape[2]
    blk = MOBA_BLOCK
    nb = L // blk
    top_k = min(MOBA_TOPK, nb - 1)
    qscale = HEAD_DIM ** -0.5 * LOG2E
    pair = pl.program_id(1)

    r_i = lax.broadcasted_iota(jnp.int32, (blk, blk), 0)
    c_i = lax.broadcasted_iota(jnp.int32, (blk, blk), 1)
    causal = r_i <= c_i
    blk_id = lax.broadcasted_iota(jnp.int32, (nb, blk), 0)
    q_row = lax.broadcasted_iota(jnp.int32, (HEAD_DIM, blk), 0)
    q_off = lax.broadcasted_iota(jnp.int32, (1, blk), 1).astype(f32)

    k_t = k_ref[0]
    ktok = k_t.T
    kmean = jnp.concatenate(
        [jnp.mean(ktok[j * blk:(j + 1) * blk], axis=0, keepdims=True) for j in range(nb)], axis=0)
    lane = lax.broadcasted_iota(jnp.int32, ktok.shape, 1)
    key_off = jnp.bitwise_and(lax.broadcasted_iota(jnp.int32, ktok.shape, 0), blk - 1).astype(f32)
    ones_rows = jnp.where(lax.broadcasted_iota(jnp.int32, (V_AUG_ROWS - HEAD_DIM, L), 0) == 0, 1.0, 0.0).astype(bf16)
    for hh in range(2):
        kh = ktok if hh == 0 else pltpu.roll(ktok, HEAD_DIM, 1)
        ktok_scr[hh] = jnp.where(lane < HEAD_DIM, kh, jnp.where(lane < HEAD_DIM + 3, key_off, 0.0)).astype(bf16)
        vaug_scr[hh, 0:HEAD_DIM, :] = v_ref[0, hh * HEAD_DIM:(hh + 1) * HEAD_DIM, :]
        vaug_scr[hh, HEAD_DIM:, :] = ones_rows

    @functools.cache
    def head_setup(hh):
        rows = slice(hh * HEAD_DIM, (hh + 1) * HEAD_DIM)
        head = pair * 2 + hh
        pieces = [jnp.full((HEAD_DIM, blk), sp_ref[head, t], f32) for t in range(3)]
        q_extra = jnp.where(q_row == 0, pieces[0], jnp.where(q_row == 1, pieces[1],
                            jnp.where(q_row == 2, pieces[2], 0.0))).astype(bf16)
        slope2 = sum(jnp.full((1, blk), sp_ref[head, t], f32) for t in range(3))
        kmax = []
        if single_pass:
            kh_t = k_t[rows, :]
            kn2 = jnp.sum(kh_t * kh_t, axis=0, keepdims=True)
            for j in range(nb):
                bj = jnp.sqrt(jnp.max(kn2[:, j * blk:(j + 1) * blk], axis=1, keepdims=True))
                kmax.append(bj if j == 0 else jnp.maximum(kmax[-1], bj))
        return rows, q_extra, slope2, kmax

    @functools.cache
    def block_setup(hh, i):
        rows, q_extra, slope2, kmax = head_setup(hh)
        qf = q_ref[0, rows, i * blk:(i + 1) * blk]
        q_aug = jnp.concatenate([(qf * qscale).astype(bf16), q_extra], axis=0)

        def scores(j):
            s = jnp.dot(ktok_scr[hh, j * blk:(j + 1) * blk, :], q_aug, preferred_element_type=f32)
            return jnp.where(causal, s, NEG) if j == i else s

        sel_off = []
        if i > 0:
            z = jnp.zeros_like(qf)
            q_pair = jnp.concatenate([qf, z] if hh == 0 else [z, qf], axis=0)
            gate = jnp.dot(kmean, q_pair, precision=HIGHEST, preferred_element_type=f32)
            rank = jnp.zeros((nb, blk), f32)
            for jp in range(i):
                row = gate[jp:jp + 1, :]
                beats = (row > gate) | ((row == gate) & (jp < blk_id))
                rank = rank + jnp.where(beats, 1.0, 0.0)
            sel = jnp.where((blk_id < i) & (rank < top_k), 1.0, 0.0)
            sel_off = [(sel[j:j + 1, :] > 0.5, slope2 * float((j - i) * blk)) for j in range(i)]
        shifts = None
        if single_pass:
            q_norm = jnp.sqrt(jnp.sum(qf * qf, axis=0, keepdims=True))
            m = (qscale * q_norm) * kmax[i] + slope2 * q_off
            shifts = [jnp.where(selj, cj - m, NEG) for selj, cj in sel_off] + [-m]
        return scores, sel_off, shifts

    def finish(hh, i, acc):
        rows = head_setup(hh)[0]
        o_ref[0, rows, i * blk:(i + 1) * blk] = acc[0:HEAD_DIM] / acc[HEAD_DIM:HEAD_DIM + 1]

    def weighted_values(hh, j, s, shift):
        p = jnp.exp2(s + shift).astype(bf16)
        return jnp.dot(vaug_scr[hh, :, j * blk:(j + 1) * blk], p, preferred_element_type=f32)

    if single_pass:
        order = [(hh, i, j) for hh in range(2) for i in range(nb) for j in range(i + 1)]
        depth = MOBA_PIPE_DEPTH

        def issue_scores(n):
            hh, i, j = order[n]
            s_scr[n % (depth + 1)] = block_setup(hh, i)[0](j)

        for n in range(depth):
            issue_scores(n)
        acc = None
        for n, (hh, i, j) in enumerate(order):
            if n + depth < len(order):
                issue_scores(n + depth)
            d = weighted_values(hh, j, s_scr[n % (depth + 1)], block_setup(hh, i)[2][j])
            acc = d if j == 0 else acc + d
            if j == i:
                finish(hh, i, acc)
    else:
        for hh in range(2):
            for i in range(nb):
                scores, sel_off, _ = block_setup(hh, i)
                s = [scores(j) for j in range(i + 1)]
                m = jnp.max(s[i], axis=0, keepdims=True)
                for j in range(i):
                    selj, cj = sel_off[j]
                    m = jnp.maximum(m, jnp.where(selj, jnp.max(s[j], axis=0, keepdims=True) + cj, NEG))
                shifts = [jnp.where(selj, cj - m, NEG) for selj, cj in sel_off] + [-m]
                acc = None
                for j in range(i + 1):
                    d = weighted_values(hh, j, s[j], shifts[j])
                    acc = d if acc is None else acc + d
                finish(hh, i, acc)


SINGLE_PASS_MAX_GAP = 64.0


def _moba(q_t, k_t, v_t, gq, gk):
    B, width, L = q_t.shape
    n_heads = width // HEAD_DIM
    assert n_heads % 2 == 0 and L % MOBA_BLOCK == 0 and MOBA_BLOCK & (MOBA_BLOCK - 1) == 0
    slope2 = jnp.exp2(-8.0 * (jnp.arange(n_heads, dtype=f32) + 1.0) / n_heads) * LOG2E
    hi = slope2.astype(bf16).astype(f32)
    mid = (slope2 - hi).astype(bf16).astype(f32)
    lo = (slope2 - hi - mid).astype(bf16).astype(f32)
    pieces = jnp.stack([hi, mid, lo], axis=1)
    spec = pl.BlockSpec((1, 2 * HEAD_DIM, L), lambda b, h: (b, h, 0))

    def call(single_pass):
        return pl.pallas_call(
            functools.partial(_moba_kernel, single_pass=single_pass),
            grid=(B, n_heads // 2),
            in_specs=[pl.BlockSpec(memory_space=pltpu.SMEM), spec, spec, spec],
            out_specs=spec,
            out_shape=jax.ShapeDtypeStruct((B, width, L), f32),
            scratch_shapes=[
                pltpu.VMEM((2, L, 2 * HEAD_DIM), bf16),
                pltpu.VMEM((2, V_AUG_ROWS, L), bf16),
                pltpu.VMEM((MOBA_PIPE_DEPTH + 1, MOBA_BLOCK, MOBA_BLOCK), f32),
            ],
            compiler_params=_params(("arbitrary", "arbitrary")),
            name="moba_single_pass" if single_pass else "moba_two_pass",
        )(pieces, q_t, k_t, v_t)

    gap = 2.0 * (HEAD_DIM ** -0.5 * LOG2E) * HEAD_DIM * jnp.max(jnp.abs(gq)) * jnp.max(jnp.abs(gk))
    return lax.cond(gap <= SINGLE_PASS_MAX_GAP, lambda: call(True), lambda: call(False))


SSM_ROW_PITCH = 24


def _gelu_tanh(x):
    c = math.sqrt(2.0 / math.pi)
    w = x * ((-2.0 * c * 0.044715 * LOG2E) * (x * x) + (-2.0 * c * LOG2E))
    return x / (1.0 + jnp.exp2(w))


def _ssm_kernel(u_ref, lrr_ref, lir_ref, lrc_ref, lic_ref, ldt_ref, btr_ref, bti_ref,
                cr_ref, ci_ref, ctr_ref, cti_ref, dexp_ref, y_ref,
                lhs_scr, m_scr, ws_scr, wo_scr, sp_scr, flat_scr):
    B = u_ref.shape[0]
    L = u_ref.shape[2]
    T = SSM_CHUNK
    P = SSM_STATE
    Hc = SSM_GROUP
    nC = L // T
    pitch = SSM_ROW_PITCH

    for c in range(nC):
        for b in range(B):
            flat_scr[(c * B + b) * pitch:(c * B + b) * pitch + Hc, :] = u_ref[b, :, c * T:(c + 1) * T]
    for c in range(nC):
        for hp in range(Hc):
            lhs_scr[c * B:(c + 1) * B, hp * T:(hp + 1) * T] = flat_scr[pl.ds(c * B * pitch + hp, B, stride=pitch), :].astype(bf16)

    dt = jnp.exp(ldt_ref[0])

    def lam_pow(lr, li, n):
        mag = jnp.exp(n * (lr * dt))
        ang = n * (li * dt)
        return mag * jnp.cos(ang), mag * jnp.sin(ang)

    lr_r, li_r = lrr_ref[0], lir_ref[0]
    lr_c, li_c = lrc_ref[0], lic_ref[0]

    ab_re, ab_im = lam_pow(lr_r, li_r, 1.0)
    den = lr_r * lr_r + li_r * li_r
    nr = ab_re - 1.0
    f_re = (nr * lr_r + ab_im * li_r) / den
    f_im = (ab_im * lr_r - nr * li_r) / den
    bb_re = f_re * btr_ref[0] - f_im * bti_ref[0]
    bb_im = f_re * bti_ref[0] + f_im * btr_ref[0]

    d_lane = lax.broadcasted_iota(jnp.int32, (P, T), 1).astype(f32)
    pd_re, pd_im = lam_pow(lr_c, li_c, d_lane)

    c_re, c_im = cr_ref[0], ci_ref[0]
    cb_re = (c_re[:, None, :] * bb_re[None, :, :] - c_im[:, None, :] * bb_im[None, :, :]).reshape(Hc * Hc, P)
    cb_im = (c_re[:, None, :] * bb_im[None, :, :] + c_im[:, None, :] * bb_re[None, :, :]).reshape(Hc * Hc, P)
    kt = (jnp.dot(cb_re, pd_re, precision=HIGHEST, preferred_element_type=f32)
          - jnp.dot(cb_im, pd_im, precision=HIGHEST, preferred_element_type=f32))

    flip = jnp.where(lax.broadcasted_iota(jnp.int32, (T, T), 0) + lax.broadcasted_iota(jnp.int32, (T, T), 1) == T - 1,
                     1.0, 0.0)
    rev_t = lambda a: lax.dot_general(flip, a, (((1,), (1,)), ((), ())), precision=HIGHEST, preferred_element_type=f32)
    pe_re, pe_im = rev_t(pd_re), rev_t(pd_im)
    for hp in range(Hc):
        br, bi = bb_re[hp:hp + 1, :], bb_im[hp:hp + 1, :]
        ws_scr[hp * T:(hp + 1) * T, :] = jnp.concatenate(
            [pe_re * br - pe_im * bi, pe_re * bi + pe_im * br], axis=1).astype(bf16)
    l1_re, l1_im = pd_re[:, 1:2], pd_im[:, 1:2]
    po_re, po_im = pd_re * l1_re - pd_im * l1_im, pd_re * l1_im + pd_im * l1_re
    ct_re, ct_im = ctr_ref[0], cti_ref[0]
    for h in range(Hc):
        cre, cim = ct_re[:, h:h + 1], ct_im[:, h:h + 1]
        wo_scr[:, h * T:(h + 1) * T] = jnp.concatenate(
            [cre * po_re - cim * po_im, -(cre * po_im + cim * po_re)], axis=0).astype(bf16)

    v = jnp.dot(lhs_scr[...], ws_scr[...], preferred_element_type=f32)
    a_re, a_im = lam_pow(lr_r, li_r, float(T))
    a1 = jnp.concatenate([a_re, a_re], axis=1)
    a2 = jnp.concatenate([-a_im, a_im], axis=1)
    v_sw = pltpu.roll(v, P, 1)
    s = jnp.zeros((B, 2 * P), f32)
    s_sw = jnp.zeros((B, 2 * P), f32)
    for c in range(nC):
        sp_scr[c * B:(c + 1) * B, :] = s.astype(bf16)
        s, s_sw = (a1 * s + a2 * s_sw + v[c * B:(c + 1) * B, :],
                   a1 * s_sw - a2 * s + v_sw[c * B:(c + 1) * B, :])

    tau_i = lax.broadcasted_iota(jnp.int32, (T, T), 0)
    t_i = lax.broadcasted_iota(jnp.int32, (T, T), 1)
    causal = t_i >= tau_i

    for h0 in range(0, Hc, 2):
        cols = slice(h0 * T, (h0 + 2) * T)
        for h in (h0, h0 + 1):
            for hp in range(Hc):
                row = kt[h * Hc + hp:h * Hc + hp + 1, :]
                tile = pltpu.roll(jnp.broadcast_to(row, (T, T)), 0, 1, stride=1, stride_axis=0)
                m_scr[hp * T:(hp + 1) * T, h * T:(h + 1) * T] = jnp.where(causal, tile, 0.0).astype(bf16)
        lhs2 = lhs_scr[:, cols]
        y = jnp.dot(lhs_scr[...], m_scr[:, cols], preferred_element_type=f32)
        y = y + jnp.dot(sp_scr[...], wo_scr[:, cols], preferred_element_type=f32)
        y = _gelu_tanh(y + dexp_ref[0, :, cols] * lhs2.astype(f32))
        for c in range(nC):
            for k, h in enumerate((h0, h0 + 1)):
                flat_scr[pl.ds(c * B * pitch + h, B, stride=pitch), :] = y[c * B:(c + 1) * B, k * T:(k + 1) * T]
    for c in range(nC):
        for b in range(B):
            y_ref[b, :, c * T:(c + 1) * T] = flat_scr[(c * B + b) * pitch:(c * B + b) * pitch + Hc, :]


def _ssm(u_t, lambda_re, lambda_im, log_dt, b_re, b_im, c_re, c_im, d_skip):
    B, width, L = u_t.shape
    G, P = lambda_re.shape
    Hc, T = SSM_GROUP, SSM_CHUNK
    assert width == G * Hc and P == SSM_STATE and L % T == 0
    nC = L // T
    row = lambda a: a.reshape(G, 1, P)
    col = lambda a: a.reshape(G, P, 1)
    d_exp = jnp.repeat(d_skip.reshape(G, 1, Hc), T, axis=2)
    args = (u_t, row(lambda_re), row(lambda_im), col(lambda_re), col(lambda_im), log_dt.reshape(G, 1, 1),
            jnp.swapaxes(b_re, 1, 2), jnp.swapaxes(b_im, 1, 2), c_re, c_im,
            jnp.swapaxes(c_re, 1, 2), jnp.swapaxes(c_im, 1, 2), d_exp)
    g3 = lambda shp: pl.BlockSpec((1,) + shp, lambda g: (g, 0, 0))
    u_spec = pl.BlockSpec((B, Hc, L), lambda g: (0, g, 0))
    return pl.pallas_call(
        _ssm_kernel,
        grid=(G,),
        in_specs=[u_spec, g3((1, P)), g3((1, P)), g3((P, 1)), g3((P, 1)), g3((1, 1)),
                  g3((Hc, P)), g3((Hc, P)), g3((Hc, P)), g3((Hc, P)), g3((P, Hc)), g3((P, Hc)),
                  g3((1, Hc * T))],
        out_specs=u_spec,
        out_shape=jax.ShapeDtypeStruct((B, width, L), f32),
        scratch_shapes=[
            pltpu.VMEM((nC * B, Hc * T), bf16),
            pltpu.VMEM((Hc * T, Hc * T), bf16),
            pltpu.VMEM((Hc * T, 2 * P), bf16),
            pltpu.VMEM((2 * P, Hc * T), bf16),
            pltpu.VMEM((nC * B, 2 * P), bf16),
            pltpu.VMEM((nC * B * SSM_ROW_PITCH, T), f32),
        ],
        compiler_params=_params(("arbitrary",)),
        name="ssm",
    )(*args)


MIX_CHUNK = 256


def _mix_kernel(x_ref, a_ref, y_ref, wg_ref, bg_ref, ga_ref, gs_ref, wo_ref, o_ref):
    def rms_rows(t, g_ref):
        return t * lax.rsqrt(jnp.mean(t * t, axis=0, keepdims=True) + EPS) * g_ref[...]

    for c in range(0, x_ref.shape[1], MIX_CHUNK):
        tok = slice(c, c + MIX_CHUNK)
        a_n = rms_rows(a_ref[0, :, tok], ga_ref)
        yg = y_ref[0, :, tok]
        z = jnp.dot(wg_ref[...], yg.astype(bf16), preferred_element_type=f32) + bg_ref[...]
        s_n = rms_rows(yg * jax.nn.sigmoid(z), gs_ref)
        mixed_t = jnp.concatenate([a_n, s_n], axis=0).astype(bf16)
        delta = lax.dot_general(mixed_t, wo_ref[...], (((0,), (0,)), ((), ())), preferred_element_type=f32)
        o_ref[0, tok, :] = x_ref[0, tok, :] + delta


def _mix(x, attn_t, yg_t, w_glu_t, b_glu, g_attn, g_ssm, w_out, tn=512):
    B, L, D = x.shape
    aw, sw = attn_t.shape[1], yg_t.shape[1]
    assert aw + sw == D and L % tn == 0 and tn % MIX_CHUNK == 0
    const = lambda shp: pl.BlockSpec(shp, lambda b, i: (0, 0), pipeline_mode=pl.Buffered(1))
    return pl.pallas_call(
        _mix_kernel,
        grid=(B, L // tn),
        in_specs=[
            pl.BlockSpec((1, tn, D), lambda b, i: (b, i, 0)),
            pl.BlockSpec((1, aw, tn), lambda b, i: (b, 0, i)),
            pl.BlockSpec((1, sw, tn), lambda b, i: (b, 0, i)),
            const((sw, sw)), const((sw, 1)), const((aw, 1)), const((sw, 1)), const((D, D)),
        ],
        out_specs=pl.BlockSpec((1, tn, D), lambda b, i: (b, i, 0)),
        out_shape=jax.ShapeDtypeStruct((B, L, D), f32),
        compiler_params=_params(("arbitrary", "arbitrary")),
        name="mix",
    )(x, attn_t, yg_t, w_glu_t, b_glu, g_attn, g_ssm, w_out)


def _ffn_kernel(x_ref, g2_ref, w1_ref, w2_ref, o_ref, h_scr):
    f = pl.program_id(1)

    @pl.when(f == 0)
    def _():
        x = x_ref[...]
        ms = jnp.mean(x * x, axis=-1, keepdims=True)
        h_scr[...] = (x * lax.rsqrt(ms + EPS) * g2_ref[...]).astype(bf16)
        o_ref[...] = x

    a = jnp.dot(h_scr[...], w1_ref[...], preferred_element_type=f32)
    a = jnp.square(jnp.maximum(a, 0.0)).astype(bf16)
    o_ref[...] += jnp.dot(a, w2_ref[...], preferred_element_type=f32)


def _ffn(x2d, g2, w1, w2, tm=512, tf=1024):
    N, D = x2d.shape
    F = w1.shape[1]
    assert N % tm == 0 and F % tf == 0
    return pl.pallas_call(
        _ffn_kernel,
        grid=(N // tm, F // tf),
        in_specs=[
            pl.BlockSpec((tm, D), lambda i, f: (i, 0)),
            pl.BlockSpec((1, D), lambda i, f: (0, 0)),
            pl.BlockSpec((D, tf), lambda i, f: (0, f)),
            pl.BlockSpec((tf, D), lambda i, f: (f, 0)),
        ],
        out_specs=pl.BlockSpec((tm, D), lambda i, f: (i, 0)),
        out_shape=jax.ShapeDtypeStruct((N, D), f32),
        scratch_shapes=[pltpu.VMEM((tm, D), bf16)],
        compiler_params=_params(("arbitrary", "arbitrary")),
        name="ffn",
    )(x2d, g2, w1, w2)


def _layer(x, norm1_gain, w_in, q_norm_gain, k_norm_gain, attn_out_gain, lambda_re, lambda_im, log_dt,
           b_re, b_im, c_re, c_im, d_skip, w_glu, b_glu, ssm_out_gain, w_out, norm2_gain, w_ff1, w_ff2):
    B, L, D = x.shape
    sw = d_skip.shape[0]
    aw = D - sw
    assert aw == sw and w_in.shape == (D, 3 * aw + sw)
    q_t, k_t, v_t, u_t = _in_proj(
        x, norm1_gain.reshape(1, D), w_in.astype(bf16),
        q_norm_gain.reshape(HEAD_DIM, 1), k_norm_gain.reshape(HEAD_DIM, 1), aw)
    attn_t = _moba(q_t, k_t, v_t, q_norm_gain, k_norm_gain)
    yg_t = _ssm(u_t, lambda_re, lambda_im, log_dt, b_re, b_im, c_re, c_im, d_skip)
    x1 = _mix(x, attn_t, yg_t, w_glu.T.astype(bf16), b_glu.reshape(sw, 1),
              attn_out_gain.reshape(aw, 1), ssm_out_gain.reshape(sw, 1), w_out.astype(bf16))
    out = _ffn(x1.reshape(B * L, D), norm2_gain.reshape(1, D), w_ff1.astype(bf16), w_ff2.astype(bf16))
    return out.reshape(B, L, D)


def kernel(x, norm1_gain, w_in, q_norm_gain, k_norm_gain, attn_out_gain, lambda_re, lambda_im, log_dt, b_re, b_im, c_re, c_im, d_skip, w_glu, b_glu, ssm_out_gain, w_out, norm2_gain, w_ff1, w_ff2):
    per_layer = (norm1_gain, w_in, q_norm_gain, k_norm_gain, attn_out_gain, lambda_re, lambda_im, log_dt,
                 b_re, b_im, c_re, c_im, d_skip, w_glu, b_glu, ssm_out_gain, w_out, norm2_gain, w_ff1, w_ff2)
    for i in range(w_in.shape[0]):
        x = _layer(x, *(p[i] for p in per_layer))
    return x
```

```python
import functools
import math

import jax
import jax.numpy as jnp
from jax import lax
from jax.experimental import pallas as pl
from jax.experimental.pallas import tpu as pltpu

f32 = jnp.float32
bf16 = jnp.bfloat16

EPS = 1e-6
NEG = -1e30
HEAD_DIM = 64
MOBA_BLOCK = 256
MOBA_TOPK = 3
SSM_GROUP = 16
SSM_STATE = 64
SSM_CHUNK = 128
VMEM_LIMIT = 56 * 1024 * 1024
HIGHEST = lax.Precision.HIGHEST


def _params(sem):
    return pltpu.CompilerParams(dimension_semantics=sem, vmem_limit_bytes=VMEM_LIMIT)


def _in_proj_kernel(x_ref, g1_ref, w_ref, gq_ref, gk_ref, q_ref, k_ref, v_ref, u_ref, h_scr):
    j = pl.program_id(1)
    i = pl.program_id(2)
    tm = x_ref.shape[1]

    @pl.when(j == 0)
    def _():
        x = x_ref[0]
        ms = jnp.mean(x * x, axis=-1, keepdims=True)
        h_scr[i] = (x * lax.rsqrt(ms + EPS) * g1_ref[...]).astype(bf16)

    def project(out_ref, finish):
        t = lax.dot_general(w_ref[...], h_scr[i], (((0,), (1,)), ((), ())), preferred_element_type=f32)
        out_ref[0] = finish(t)

    def head_norm(g_ref):
        def finish(t):
            t3 = t.reshape(t.shape[0] // HEAD_DIM, HEAD_DIM, tm)
            ms = jnp.mean(t3 * t3, axis=1, keepdims=True)
            return (t3 * lax.rsqrt(ms + EPS) * g_ref[...][None]).reshape(t.shape)
        return finish

    @pl.when(j == 0)
    def _():
        project(q_ref, head_norm(gq_ref))

    @pl.when(j == 1)
    def _():
        project(k_ref, head_norm(gk_ref))

    @pl.when(j == 2)
    def _():
        project(v_ref, lambda t: t.astype(bf16))

    @pl.when(j == 3)
    def _():
        project(u_ref, lambda t: t)


def _in_proj(x, g1, w_in, gq, gk, width, tm=512):
    B, L, D = x.shape
    assert w_in.shape == (D, 4 * width) and L % tm == 0
    n_i = L // tm
    last = n_i - 1

    def out_spec(jj):
        return pl.BlockSpec((1, width, tm),
                            lambda b, j, i: (b, 0, jnp.where(j == jj, i, jnp.where(j < jj, 0, last))))

    return pl.pallas_call(
        _in_proj_kernel,
        grid=(B, 4, n_i),
        in_specs=[
            pl.BlockSpec((1, tm, D), lambda b, j, i: (b, jnp.where(j == 0, i, last), 0)),
            pl.BlockSpec((1, D), lambda b, j, i: (0, 0)),
            pl.BlockSpec((D, width), lambda b, j, i: (0, j)),
            pl.BlockSpec((HEAD_DIM, 1), lambda b, j, i: (0, 0)),
            pl.BlockSpec((HEAD_DIM, 1), lambda b, j, i: (0, 0)),
        ],
        out_specs=[out_spec(0), out_spec(1), out_spec(2), out_spec(3)],
        out_shape=[
            jax.ShapeDtypeStruct((B, width, L), f32),
            jax.ShapeDtypeStruct((B, width, L), f32),
            jax.ShapeDtypeStruct((B, width, L), bf16),
            jax.ShapeDtypeStruct((B, width, L), f32),
        ],
        scratch_shapes=[pltpu.VMEM((n_i, tm, D), bf16)],
        compiler_params=_params(("arbitrary", "arbitrary", "arbitrary")),
        name="in_proj",
    )(x, g1, w_in, gq, gk)


LOG2E = 1.4426950408889634
V_AUG_ROWS = HEAD_DIM + 16
MOBA_PIPE_DEPTH = 5


def _moba_kernel(sp_ref, q_ref, k_ref, v_ref, o_ref, ktok_scr, vaug_scr, s_scr, *, single_pass):
    L = q_ref.shape[2]
    blk = MOBA_BLOCK
    nb = L // blk
    top_k = min(MOBA_TOPK, nb - 1)
    qscale = HEAD_DIM ** -0.5 * LOG2E
    pair = pl.program_id(1)

    r_i = lax.broadcasted_iota(jnp.int32, (blk, blk), 0)
    c_i = lax.broadcasted_iota(jnp.int32, (blk, blk), 1)
    causal = r_i <= c_i
    blk_id = lax.broadcasted_iota(jnp.int32, (nb, blk), 0)
    q_row = lax.broadcasted_iota(jnp.int32, (HEAD_DIM, blk), 0)
    q_off = lax.broadcasted_iota(jnp.int32, (1, blk), 1).astype(f32)

    k_t = k_ref[0]
    ktok = k_t.T
    kmean = jnp.concatenate(
        [jnp.mean(ktok[j * blk:(j + 1) * blk], axis=0, keepdims=True) for j in range(nb)], axis=0)
    lane = lax.broadcasted_iota(jnp.int32, ktok.shape, 1)
    key_off = jnp.bitwise_and(lax.broadcasted_iota(jnp.int32, ktok.shape, 0), blk - 1).astype(f32)
    ones_rows = jnp.where(lax.broadcasted_iota(jnp.int32, (V_AUG_ROWS - HEAD_DIM, L), 0) == 0, 1.0, 0.0).astype(bf16)
    for hh in range(2):
        kh = ktok if hh == 0 else pltpu.roll(ktok, HEAD_DIM, 1)
        ktok_scr[hh] = jnp.where(lane < HEAD_DIM, kh, jnp.where(lane < HEAD_DIM + 3, key_off, 0.0)).astype(bf16)
        vaug_scr[hh, 0:HEAD_DIM, :] = v_ref[0, hh * HEAD_DIM:(hh + 1) * HEAD_DIM, :]
        vaug_scr[hh, HEAD_DIM:, :] = ones_rows

    @functools.cache
    def head_setup(hh):
        rows = slice(hh * HEAD_DIM, (hh + 1) * HEAD_DIM)
        head = pair * 2 + hh
        pieces = [jnp.full((HEAD_DIM, blk), sp_ref[head, t], f32) for t in range(3)]
        q_extra = jnp.where(q_row == 0, pieces[0], jnp.where(q_row == 1, pieces[1],
                            jnp.where(q_row == 2, pieces[2], 0.0))).astype(bf16)
        slope2 = sum(jnp.full((1, blk), sp_ref[head, t], f32) for t in range(3))
        kmax = []
        if single_pass:
            kh_t = k_t[rows, :]
            kn2 = jnp.sum(kh_t * kh_t, axis=0, keepdims=True)
            for j in range(nb):
                bj = jnp.sqrt(jnp.max(kn2[:, j * blk:(j + 1) * blk], axis=1, keepdims=True))
                kmax.append(bj if j == 0 else jnp.maximum(kmax[-1], bj))
        return rows, q_extra, slope2, kmax

    @functools.cache
    def block_setup(hh, i):
        rows, q_extra, slope2, kmax = head_setup(hh)
        qf = q_ref[0, rows, i * blk:(i + 1) * blk]
        q_aug = jnp.concatenate([(qf * qscale).astype(bf16), q_extra], axis=0)

        def scores(j):
            s = jnp.dot(ktok_scr[hh, j * blk:(j + 1) * blk, :], q_aug, preferred_element_type=f32)
            return jnp.where(causal, s, NEG) if j == i else s

        sel_off = []
        if i > 0:
            z = jnp.zeros_like(qf)
            q_pair = jnp.concatenate([qf, z] if hh == 0 else [z, qf], axis=0)
            gate = jnp.dot(kmean, q_pair, precision=HIGHEST, preferred_element_type=f32)
            rank = jnp.zeros((nb, blk), f32)
            for jp in range(i):
                row = gate[jp:jp + 1, :]
                beats = (row > gate) | ((row == gate) & (jp < blk_id))
                rank = rank + jnp.where(beats, 1.0, 0.0)
            sel = jnp.where((blk_id < i) & (rank < top_k), 1.0, 0.0)
            sel_off = [(sel[j:j + 1, :] > 0.5, slope2 * float((j - i) * blk)) for j in range(i)]
        shifts = None
        if single_pass:
            q_norm = jnp.sqrt(jnp.sum(qf * qf, axis=0, keepdims=True))
            m = (qscale * q_norm) * kmax[i] + slope2 * q_off
            shifts = [jnp.where(selj, cj - m, NEG) for selj, cj in sel_off] + [-m]
        return scores, sel_off, shifts

    def finish(hh, i, acc):
        rows = head_setup(hh)[0]
        o_ref[0, rows, i * blk:(i + 1) * blk] = acc[0:HEAD_DIM] / acc[HEAD_DIM:HEAD_DIM + 1]

    def weighted_values(hh, j, s, shift):
        p = jnp.exp2(s + shift).astype(bf16)
        return jnp.dot(vaug_scr[hh, :, j * blk:(j + 1) * blk], p, preferred_element_type=f32)

    if single_pass:
        order = [(hh, i, j) for hh in range(2) for i in range(nb) for j in range(i + 1)]
        depth = MOBA_PIPE_DEPTH

        def issue_scores(n):
            hh, i, j = order[n]
            s_scr[n % (depth + 1)] = block_setup(hh, i)[0](j)

        for n in range(depth):
            issue_scores(n)
        acc = None
        for n, (hh, i, j) in enumerate(order):
            if n + depth < len(order):
                issue_scores(n + depth)
            d = weighted_values(hh, j, s_scr[n % (depth + 1)], block_setup(hh, i)[2][j])
            acc = d if j == 0 else acc + d
            if j == i:
                finish(hh, i, acc)
    else:
        for hh in range(2):
            for i in range(nb):
                scores, sel_off, _ = block_setup(hh, i)
                s = [scores(j) for j in range(i + 1)]
                m = jnp.max(s[i], axis=0, keepdims=True)
                for j in range(i):
                    selj, cj = sel_off[j]
                    m = jnp.maximum(m, jnp.where(selj, jnp.max(s[j], axis=0, keepdims=True) + cj, NEG))
                shifts = [jnp.where(selj, cj - m, NEG) for selj, cj in sel_off] + [-m]
                acc = None
                for j in range(i + 1):
                    d = weighted_values(hh, j, s[j], shifts[j])
                    acc = d if acc is None else acc + d
                finish(hh, i, acc)


SINGLE_PASS_MAX_GAP = 64.0


def _moba(q_t, k_t, v_t, gq, gk):
    B, width, L = q_t.shape
    n_heads = width // HEAD_DIM
    assert n_heads % 2 == 0 and L % MOBA_BLOCK == 0 and MOBA_BLOCK & (MOBA_BLOCK - 1) == 0
    slope2 = jnp.exp2(-8.0 * (jnp.arange(n_heads, dtype=f32) + 1.0) / n_heads) * LOG2E
    hi = slope2.astype(bf16).astype(f32)
    mid = (slope2 - hi).astype(bf16).astype(f32)
    lo = (slope2 - hi - mid).astype(bf16).astype(f32)
    pieces = jnp.stack([hi, mid, lo], axis=1)
    spec = pl.BlockSpec((1, 2 * HEAD_DIM, L), lambda b, h: (b, h, 0))

    def call(single_pass):
        return pl.pallas_call(
            functools.partial(_moba_kernel, single_pass=single_pass),
            grid=(B, n_heads // 2),
            in_specs=[pl.BlockSpec(memory_space=pltpu.SMEM), spec, spec, spec],
            out_specs=spec,
            out_shape=jax.ShapeDtypeStruct((B, width, L), f32),
            scratch_shapes=[
                pltpu.VMEM((2, L, 2 * HEAD_DIM), bf16),
                pltpu.VMEM((2, V_AUG_ROWS, L), bf16),
                pltpu.VMEM((MOBA_PIPE_DEPTH + 1, MOBA_BLOCK, MOBA_BLOCK), f32),
            ],
            compiler_params=_params(("arbitrary", "arbitrary")),
            name="moba_single_pass" if single_pass else "moba_two_pass",
        )(pieces, q_t, k_t, v_t)

    gap = 2.0 * (HEAD_DIM ** -0.5 * LOG2E) * HEAD_DIM * jnp.max(jnp.abs(gq)) * jnp.max(jnp.abs(gk))
    return lax.cond(gap <= SINGLE_PASS_MAX_GAP, lambda: call(True), lambda: call(False))


SSM_ROW_PITCH = 24


def _gelu_tanh(x):
    c = math.sqrt(2.0 / math.pi)
    w = x * ((-2.0 * c * 0.044715 * LOG2E) * (x * x) + (-2.0 * c * LOG2E))
    return x / (1.0 + jnp.exp2(w))


def _ssm_kernel(u_ref, par_ref, dexp_ref, y_ref, lhs_scr, m_scr, ws_scr, wo_scr, sp_scr, flat_scr):
    B = u_ref.shape[0]
    L = u_ref.shape[2]
    T = SSM_CHUNK
    P = SSM_STATE
    Hc = SSM_GROUP
    nC = L // T
    pitch = SSM_ROW_PITCH

    for c in range(nC):
        for b in range(B):
            flat_scr[(c * B + b) * pitch:(c * B + b) * pitch + Hc, :] = u_ref[b, :, c * T:(c + 1) * T]
    for c in range(nC):
        for hp in range(Hc):
            lhs_scr[c * B:(c + 1) * B, hp * T:(hp + 1) * T] = flat_scr[pl.ds(c * B * pitch + hp, B, stride=pitch), :].astype(bf16)

    P2 = 2 * P
    lr_r, li_r = par_ref[0, 0:1, 0:P], par_ref[0, 1:2, 0:P]
    dt = jnp.exp(par_ref[0, 2:3, 0:1])
    bt_re, bt_im = par_ref[0, 8:8 + Hc, 0:P], par_ref[0, 8:8 + Hc, P:P2]
    c_re, c_im = par_ref[0, 8 + Hc:8 + 2 * Hc, 0:P], par_ref[0, 8 + Hc:8 + 2 * Hc, P:P2]
    col0 = 8 + 2 * Hc
    lr_c, li_c = par_ref[0, col0:col0 + P, 0:1], par_ref[0, col0:col0 + P, 1:2]
    ct_re, ct_im = par_ref[0, col0:col0 + P, Hc:2 * Hc], par_ref[0, col0:col0 + P, 2 * Hc:3 * Hc]

    def lam_pow(lr, li, n):
        mag = jnp.exp(n * (lr * dt))
        ang = n * (li * dt)
        return mag * jnp.cos(ang), mag * jnp.sin(ang)

    ab_re, ab_im = lam_pow(lr_r, li_r, 1.0)
    den = lr_r * lr_r + li_r * li_r
    nr = ab_re - 1.0
    f_re = (nr * lr_r + ab_im * li_r) / den
    f_im = (ab_im * lr_r - nr * li_r) / den
    bb_re = f_re * bt_re - f_im * bt_im
    bb_im = f_re * bt_im + f_im * bt_re

    d_lane = lax.broadcasted_iota(jnp.int32, (P, T), 1).astype(f32)
    pd_re, pd_im = lam_pow(lr_c, li_c, d_lane)

    cb_re = (c_re[:, None, :] * bb_re[None, :, :] - c_im[:, None, :] * bb_im[None, :, :]).reshape(Hc * Hc, P)
    cb_im = (c_re[:, None, :] * bb_im[None, :, :] + c_im[:, None, :] * bb_re[None, :, :]).reshape(Hc * Hc, P)
    kt = (jnp.dot(cb_re, pd_re, precision=HIGHEST, preferred_element_type=f32)
          - jnp.dot(cb_im, pd_im, precision=HIGHEST, preferred_element_type=f32))

    flip = jnp.where(lax.broadcasted_iota(jnp.int32, (T, T), 0) + lax.broadcasted_iota(jnp.int32, (T, T), 1) == T - 1,
                     1.0, 0.0)
    rev_t = lambda a: lax.dot_general(flip, a, (((1,), (1,)), ((), ())), precision=HIGHEST, preferred_element_type=f32)
    pe_re, pe_im = rev_t(pd_re), rev_t(pd_im)
    for hp in range(Hc):
        br, bi = bb_re[hp:hp + 1, :], bb_im[hp:hp + 1, :]
        ws_scr[hp * T:(hp + 1) * T, :] = jnp.concatenate(
            [pe_re * br - pe_im * bi, pe_re * bi + pe_im * br], axis=1).astype(bf16)
    l1_re, l1_im = pd_re[:, 1:2], pd_im[:, 1:2]
    po_re, po_im = pd_re * l1_re - pd_im * l1_im, pd_re * l1_im + pd_im * l1_re
    for h in range(Hc):
        cre, cim = ct_re[:, h:h + 1], ct_im[:, h:h + 1]
        wo_scr[:, h * T:(h + 1) * T] = jnp.concatenate(
            [cre * po_re - cim * po_im, -(cre * po_im + cim * po_re)], axis=0).astype(bf16)

    v = jnp.dot(lhs_scr[...], ws_scr[...], preferred_element_type=f32)
    a_re, a_im = lam_pow(lr_r, li_r, float(T))
    a1 = jnp.concatenate([a_re, a_re], axis=1)
    a2 = jnp.concatenate([-a_im, a_im], axis=1)
    v_sw = pltpu.roll(v, P, 1)
    s = jnp.zeros((B, 2 * P), f32)
    s_sw = jnp.zeros((B, 2 * P), f32)
    for c in range(nC):
        sp_scr[c * B:(c + 1) * B, :] = s.astype(bf16)
        s, s_sw = (a1 * s + a2 * s_sw + v[c * B:(c + 1) * B, :],
                   a1 * s_sw - a2 * s + v_sw[c * B:(c + 1) * B, :])

    tau_i = lax.broadcasted_iota(jnp.int32, (T, T), 0)
    t_i = lax.broadcasted_iota(jnp.int32, (T, T), 1)
    causal = t_i >= tau_i

    for h0 in range(0, Hc, 2):
        cols = slice(h0 * T, (h0 + 2) * T)
        for h in (h0, h0 + 1):
            for hp in range(Hc):
                row = kt[h * Hc + hp:h * Hc + hp + 1, :]
                tile = pltpu.roll(jnp.broadcast_to(row, (T, T)), 0, 1, stride=1, stride_axis=0)
                m_scr[hp * T:(hp + 1) * T, h * T:(h + 1) * T] = jnp.where(causal, tile, 0.0).astype(bf16)
        lhs2 = lhs_scr[:, cols]
        y = jnp.dot(lhs_scr[...], m_scr[:, cols], preferred_element_type=f32)
        y = y + jnp.dot(sp_scr[...], wo_scr[:, cols], preferred_element_type=f32)
        y = _gelu_tanh(y + dexp_ref[0, :, cols] * lhs2.astype(f32))
        for c in range(nC):
            for k, h in enumerate((h0, h0 + 1)):
                flat_scr[pl.ds(c * B * pitch + h, B, stride=pitch), :] = y[c * B:(c + 1) * B, k * T:(k + 1) * T]
    for c in range(nC):
        for b in range(B):
            y_ref[b, :, c * T:(c + 1) * T] = flat_scr[(c * B + b) * pitch:(c * B + b) * pitch + Hc, :]


def _ssm_pack(lambda_re, lambda_im, log_dt, b_re, b_im, c_re, c_im):
    G, P = lambda_re.shape
    Hc = SSM_GROUP
    assert 2 * P == 128 and 3 * Hc <= 128
    lanes = lambda a: jnp.pad(a, ((0, 0), (0, 0), (0, 128 - a.shape[2])))
    head = jnp.pad(jnp.stack([lanes(lambda_re[:, None, :])[:, 0], lanes(lambda_im[:, None, :])[:, 0],
                              lanes(log_dt[:, None, None])[:, 0]], axis=1), ((0, 0), (0, 5), (0, 0)))
    b_t = jnp.concatenate([jnp.swapaxes(b_re, 1, 2), jnp.swapaxes(b_im, 1, 2)], axis=2)
    c_n = jnp.concatenate([c_re, c_im], axis=2)
    cols = jnp.concatenate([lambda_re[:, :, None], lambda_im[:, :, None], jnp.zeros((G, P, Hc - 2), f32),
                            jnp.swapaxes(c_re, 1, 2), jnp.swapaxes(c_im, 1, 2)], axis=2)
    return jnp.concatenate([head, b_t, c_n, lanes(cols)], axis=1)


def _ssm(u_t, lambda_re, lambda_im, log_dt, b_re, b_im, c_re, c_im, d_skip):
    B, width, L = u_t.shape
    G, P = lambda_re.shape
    Hc, T = SSM_GROUP, SSM_CHUNK
    assert width == G * Hc and P == SSM_STATE and L % T == 0
    nC = L // T
    params = _ssm_pack(lambda_re, lambda_im, log_dt, b_re, b_im, c_re, c_im)
    d_exp = jnp.repeat(d_skip.reshape(G, 1, Hc), T, axis=2)
    g3 = lambda shp: pl.BlockSpec((1,) + shp, lambda g: (g, 0, 0))
    u_spec = pl.BlockSpec((B, Hc, L), lambda g: (0, g, 0))
    return pl.pallas_call(
        _ssm_kernel,
        grid=(G,),
        in_specs=[u_spec, g3(params.shape[1:]), g3((1, Hc * T))],
        out_specs=u_spec,
        out_shape=jax.ShapeDtypeStruct((B, width, L), f32),
        scratch_shapes=[
            pltpu.VMEM((nC * B, Hc * T), bf16),
            pltpu.VMEM((Hc * T, Hc * T), bf16),
            pltpu.VMEM((Hc * T, 2 * P), bf16),
            pltpu.VMEM((2 * P, Hc * T), bf16),
            pltpu.VMEM((nC * B, 2 * P), bf16),
            pltpu.VMEM((nC * B * SSM_ROW_PITCH, T), f32),
        ],
        compiler_params=_params(("arbitrary",)),
        name="ssm",
    )(u_t, params, d_exp)


MIX_CHUNK = 256


def _mix_kernel(x_ref, a_ref, y_ref, wg_ref, bg_ref, ga_ref, gs_ref, wo_ref, o_ref):
    def rms_rows(t, g_ref):
        return t * lax.rsqrt(jnp.mean(t * t, axis=0, keepdims=True) + EPS) * g_ref[...]

    for c in range(0, x_ref.shape[1], MIX_CHUNK):
        tok = slice(c, c + MIX_CHUNK)
        a_n = rms_rows(a_ref[0, :, tok], ga_ref)
        yg = y_ref[0, :, tok]
        z = jnp.dot(wg_ref[...], yg.astype(bf16), preferred_element_type=f32) + bg_ref[...]
        s_n = rms_rows(yg * jax.nn.sigmoid(z), gs_ref)
        mixed_t = jnp.concatenate([a_n, s_n], axis=0).astype(bf16)
        delta = lax.dot_general(mixed_t, wo_ref[...], (((0,), (0,)), ((), ())), preferred_element_type=f32)
        o_ref[0, tok, :] = x_ref[0, tok, :] + delta


def _mix(x, attn_t, yg_t, w_glu_t, b_glu, g_attn, g_ssm, w_out, tn=512):
    B, L, D = x.shape
    aw, sw = attn_t.shape[1], yg_t.shape[1]
    assert aw + sw == D and L % tn == 0 and tn % MIX_CHUNK == 0
    const = lambda shp: pl.BlockSpec(shp, lambda b, i: (0, 0), pipeline_mode=pl.Buffered(1))
    return pl.pallas_call(
        _mix_kernel,
        grid=(B, L // tn),
        in_specs=[
            pl.BlockSpec((1, tn, D), lambda b, i: (b, i, 0)),
            pl.BlockSpec((1, aw, tn), lambda b, i: (b, 0, i)),
            pl.BlockSpec((1, sw, tn), lambda b, i: (b, 0, i)),
            const((sw, sw)), const((sw, 1)), const((aw, 1)), const((sw, 1)), const((D, D)),
        ],
        out_specs=pl.BlockSpec((1, tn, D), lambda b, i: (b, i, 0)),
        out_shape=jax.ShapeDtypeStruct((B, L, D), f32),
        compiler_params=_params(("arbitrary", "arbitrary")),
        name="mix",
    )(x, attn_t, yg_t, w_glu_t, b_glu, g_attn, g_ssm, w_out)


def _ffn_kernel(x_ref, g2_ref, w1_ref, w2_ref, o_ref, h_scr):
    f = pl.program_id(1)

    @pl.when(f == 0)
    def _():
        x = x_ref[...]
        ms = jnp.mean(x * x, axis=-1, keepdims=True)
        h_scr[...] = (x * lax.rsqrt(ms + EPS) * g2_ref[...]).astype(bf16)
        o_ref[...] = x

    a = jnp.dot(h_scr[...], w1_ref[...], preferred_element_type=f32)
    a = jnp.square(jnp.maximum(a, 0.0)).astype(bf16)
    o_ref[...] += jnp.dot(a, w2_ref[...], preferred_element_type=f32)


def _ffn(x2d, g2, w1, w2, tm=512, tf=1024):
    N, D = x2d.shape
    F = w1.shape[1]
    assert N % tm == 0 and F % tf == 0
    return pl.pallas_call(
        _ffn_kernel,
        grid=(N // tm, F // tf),
        in_specs=[
            pl.BlockSpec((tm, D), lambda i, f: (i, 0)),
            pl.BlockSpec((1, D), lambda i, f: (0, 0)),
            pl.BlockSpec((D, tf), lambda i, f: (0, f)),
            pl.BlockSpec((tf, D), lambda i, f: (f, 0)),
        ],
        out_specs=pl.BlockSpec((tm, D), lambda i, f: (i, 0)),
        out_shape=jax.ShapeDtypeStruct((N, D), f32),
        scratch_shapes=[pltpu.VMEM((tm, D), bf16)],
        compiler_params=_params(("arbitrary", "arbitrary")),
        name="ffn",
    )(x2d, g2, w1, w2)


def _layer(x, norm1_gain, w_in, q_norm_gain, k_norm_gain, attn_out_gain, lambda_re, lambda_im, log_dt,
           b_re, b_im, c_re, c_im, d_skip, w_glu, b_glu, ssm_out_gain, w_out, norm2_gain, w_ff1, w_ff2):
    B, L, D = x.shape
    sw = d_skip.shape[0]
    aw = D - sw
    assert aw == sw and w_in.shape == (D, 3 * aw + sw)
    q_t, k_t, v_t, u_t = _in_proj(
        x, norm1_gain.reshape(1, D), w_in.astype(bf16),
        q_norm_gain.reshape(HEAD_DIM, 1), k_norm_gain.reshape(HEAD_DIM, 1), aw)
    attn_t = _moba(q_t, k_t, v_t, q_norm_gain, k_norm_gain)
    yg_t = _ssm(u_t, lambda_re, lambda_im, log_dt, b_re, b_im, c_re, c_im, d_skip)
    x1 = _mix(x, attn_t, yg_t, w_glu.T.astype(bf16), b_glu.reshape(sw, 1),
              attn_out_gain.reshape(aw, 1), ssm_out_gain.reshape(sw, 1), w_out.astype(bf16))
    out = _ffn(x1.reshape(B * L, D), norm2_gain.reshape(1, D), w_ff1.astype(bf16), w_ff2.astype(bf16))
    return out.reshape(B, L, D)


def kernel(x, norm1_gain, w_in, q_norm_gain, k_norm_gain, attn_out_gain, lambda_re, lambda_im, log_dt, b_re, b_im, c_re, c_im, d_skip, w_glu, b_glu, ssm_out_gain, w_out, norm2_gain, w_ff1, w_ff2):
    per_layer = (norm1_gain, w_in, q_norm_gain, k_norm_gain, attn_out_gain, lambda_re, lambda_im, log_dt,
                 b_re, b_im, c_re, c_im, d_skip, w_glu, b_glu, ssm_out_gain, w_out, norm2_gain, w_ff1, w_ff2)
    for i in range(w_in.shape[0]):
        x = _layer(x, *(p[i] for p in per_layer))
    return x
```

```python
import functools
import math

import jax
import jax.numpy as jnp
from jax import lax
from jax.experimental import pallas as pl
from jax.experimental.pallas import tpu as pltpu

f32 = jnp.float32
bf16 = jnp.bfloat16

EPS = 1e-6
NEG = -1e30
HEAD_DIM = 64
MOBA_BLOCK = 256
MOBA_TOPK = 3
SSM_GROUP = 16
SSM_STATE = 64
SSM_CHUNK = 128
VMEM_LIMIT = 56 * 1024 * 1024
HIGHEST = lax.Precision.HIGHEST


def _params(sem):
    return pltpu.CompilerParams(dimension_semantics=sem, vmem_limit_bytes=VMEM_LIMIT)


def _in_proj_kernel(x_ref, g1_ref, w_ref, gq_ref, gk_ref, q_ref, k_ref, v_ref, u_ref, h_scr):
    j = pl.program_id(1)
    i = pl.program_id(2)
    tm = x_ref.shape[1]

    @pl.when(j == 0)
    def _():
        x = x_ref[0]
        ms = jnp.mean(x * x, axis=-1, keepdims=True)
        h_scr[i] = (x * lax.rsqrt(ms + EPS) * g1_ref[...]).astype(bf16)

    def project(out_ref, finish):
        t = lax.dot_general(w_ref[...], h_scr[i], (((0,), (1,)), ((), ())), preferred_element_type=f32)
        out_ref[0] = finish(t)

    def head_norm(g_ref):
        def finish(t):
            t3 = t.reshape(t.shape[0] // HEAD_DIM, HEAD_DIM, tm)
            ms = jnp.mean(t3 * t3, axis=1, keepdims=True)
            return (t3 * lax.rsqrt(ms + EPS) * g_ref[...][None]).reshape(t.shape)
        return finish

    @pl.when(j == 0)
    def _():
        project(q_ref, head_norm(gq_ref))

    @pl.when(j == 1)
    def _():
        project(k_ref, head_norm(gk_ref))

    @pl.when(j == 2)
    def _():
        project(v_ref, lambda t: t.astype(bf16))

    @pl.when(j == 3)
    def _():
        project(u_ref, lambda t: t)


def _in_proj(x, g1, w_in, gq, gk, width, tm=512):
    B, L, D = x.shape
    assert w_in.shape == (D, 4 * width) and L % tm == 0
    n_i = L // tm
    last = n_i - 1

    def out_spec(jj):
        return pl.BlockSpec((1, width, tm),
                            lambda b, j, i: (b, 0, jnp.where(j == jj, i, jnp.where(j < jj, 0, last))))

    return pl.pallas_call(
        _in_proj_kernel,
        grid=(B, 4, n_i),
        in_specs=[
            pl.BlockSpec((1, tm, D), lambda b, j, i: (b, jnp.where(j == 0, i, last), 0)),
            pl.BlockSpec((1, D), lambda b, j, i: (0, 0)),
            pl.BlockSpec((D, width), lambda b, j, i: (0, j)),
            pl.BlockSpec((HEAD_DIM, 1), lambda b, j, i: (0, 0)),
            pl.BlockSpec((HEAD_DIM, 1), lambda b, j, i: (0, 0)),
        ],
        out_specs=[out_spec(0), out_spec(1), out_spec(2), out_spec(3)],
        out_shape=[
            jax.ShapeDtypeStruct((B, width, L), f32),
            jax.ShapeDtypeStruct((B, width, L), f32),
            jax.ShapeDtypeStruct((B, width, L), bf16),
            jax.ShapeDtypeStruct((B, width, L), f32),
        ],
        scratch_shapes=[pltpu.VMEM((n_i, tm, D), bf16)],
        compiler_params=_params(("arbitrary", "arbitrary", "arbitrary")),
        name="in_proj",
    )(x, g1, w_in, gq, gk)


LOG2E = 1.4426950408889634
V_AUG_ROWS = HEAD_DIM + 16
MOBA_PIPE_DEPTH = 5


def _moba_kernel(sp_ref, q_ref, k_ref, v_ref, o_ref, ktok_scr, vaug_scr, s_scr, *, single_pass):
    L = q_ref.shape[2]
    blk = MOBA_BLOCK
    nb = L // blk
    top_k = min(MOBA_TOPK, nb - 1)
    qscale = HEAD_DIM ** -0.5 * LOG2E
    pair = pl.program_id(1)

    r_i = lax.broadcasted_iota(jnp.int32, (blk, blk), 0)
    c_i = lax.broadcasted_iota(jnp.int32, (blk, blk), 1)
    causal = r_i <= c_i
    blk_id = lax.broadcasted_iota(jnp.int32, (nb, blk), 0)
    q_row = lax.broadcasted_iota(jnp.int32, (HEAD_DIM, blk), 0)
    q_off = lax.broadcasted_iota(jnp.int32, (1, blk), 1).astype(f32)

    k_t = k_ref[0]
    ktok = k_t.T
    kmean = jnp.concatenate(
        [jnp.mean(ktok[j * blk:(j + 1) * blk], axis=0, keepdims=True) for j in range(nb)], axis=0)
    lane = lax.broadcasted_iota(jnp.int32, ktok.shape, 1)
    key_off = jnp.bitwise_and(lax.broadcasted_iota(jnp.int32, ktok.shape, 0), blk - 1).astype(f32)
    ones_rows = jnp.where(lax.broadcasted_iota(jnp.int32, (V_AUG_ROWS - HEAD_DIM, L), 0) == 0, 1.0, 0.0).astype(bf16)
    for hh in range(2):
        kh = ktok if hh == 0 else pltpu.roll(ktok, HEAD_DIM, 1)
        ktok_scr[hh] = jnp.where(lane < HEAD_DIM, kh, jnp.where(lane < HEAD_DIM + 3, key_off, 0.0)).astype(bf16)
        vaug_scr[hh, 0:HEAD_DIM, :] = v_ref[0, hh * HEAD_DIM:(hh + 1) * HEAD_DIM, :]
        vaug_scr[hh, HEAD_DIM:, :] = ones_rows

    @functools.cache
    def head_setup(hh):
        rows = slice(hh * HEAD_DIM, (hh + 1) * HEAD_DIM)
        head = pair * 2 + hh
        pieces = [jnp.full((HEAD_DIM, blk), sp_ref[head, t], f32) for t in range(3)]
        q_extra = jnp.where(q_row == 0, pieces[0], jnp.where(q_row == 1, pieces[1],
                            jnp.where(q_row == 2, pieces[2], 0.0))).astype(bf16)
        slope2 = sum(jnp.full((1, blk), sp_ref[head, t], f32) for t in range(3))
        kmax = []
        if single_pass:
            kh_t = k_t[rows, :]
            kn2 = jnp.sum(kh_t * kh_t, axis=0, keepdims=True)
            for j in range(nb):
                bj = jnp.sqrt(jnp.max(kn2[:, j * blk:(j + 1) * blk], axis=1, keepdims=True))
                kmax.append(bj if j == 0 else jnp.maximum(kmax[-1], bj))
        return rows, q_extra, slope2, kmax

    @functools.cache
    def block_setup(hh, i):
        rows, q_extra, slope2, kmax = head_setup(hh)
        qf = q_ref[0, rows, i * blk:(i + 1) * blk]
        q_aug = jnp.concatenate([(qf * qscale).astype(bf16), q_extra], axis=0)

        def scores(j):
            s = jnp.dot(ktok_scr[hh, j * blk:(j + 1) * blk, :], q_aug, preferred_element_type=f32)
            return jnp.where(causal, s, NEG) if j == i else s

        sel_off = []
        if i > 0:
            z = jnp.zeros_like(qf)
            q_pair = jnp.concatenate([qf, z] if hh == 0 else [z, qf], axis=0)
            gate = jnp.dot(kmean, q_pair, precision=HIGHEST, preferred_element_type=f32)
            rank = jnp.zeros((nb, blk), f32)
            for jp in range(i):
                row = gate[jp:jp + 1, :]
                beats = (row > gate) | ((row == gate) & (jp < blk_id))
                rank = rank + jnp.where(beats, 1.0, 0.0)
            sel = jnp.where((blk_id < i) & (rank < top_k), 1.0, 0.0)
            sel_off = [(sel[j:j + 1, :] > 0.5, slope2 * float((j - i) * blk)) for j in range(i)]
        shifts = None
        if single_pass:
            q_norm = jnp.sqrt(jnp.sum(qf * qf, axis=0, keepdims=True))
            m = (qscale * q_norm) * kmax[i] + slope2 * q_off
            shifts = [jnp.where(selj, cj - m, NEG) for selj, cj in sel_off] + [-m]
        return scores, sel_off, shifts

    def finish(hh, i, acc):
        rows = head_setup(hh)[0]
        o_ref[0, rows, i * blk:(i + 1) * blk] = (acc[0:HEAD_DIM] / acc[HEAD_DIM:HEAD_DIM + 1]).astype(o_ref.dtype)

    def weighted_values(hh, j, s, shift):
        p = jnp.exp2(s + shift).astype(bf16)
        return jnp.dot(vaug_scr[hh, :, j * blk:(j + 1) * blk], p, preferred_element_type=f32)

    if single_pass:
        order = [(hh, i, j) for hh in range(2) for i in range(nb) for j in range(i + 1)]
        depth = MOBA_PIPE_DEPTH

        def issue_scores(n):
            hh, i, j = order[n]
            s_scr[n % (depth + 1)] = block_setup(hh, i)[0](j)

        for n in range(depth):
            issue_scores(n)
        acc = None
        for n, (hh, i, j) in enumerate(order):
            if n + depth < len(order):
                issue_scores(n + depth)
            d = weighted_values(hh, j, s_scr[n % (depth + 1)], block_setup(hh, i)[2][j])
            acc = d if j == 0 else acc + d
            if j == i:
                finish(hh, i, acc)
    else:
        for hh in range(2):
            for i in range(nb):
                scores, sel_off, _ = block_setup(hh, i)
                s = [scores(j) for j in range(i + 1)]
                m = jnp.max(s[i], axis=0, keepdims=True)
                for j in range(i):
                    selj, cj = sel_off[j]
                    m = jnp.maximum(m, jnp.where(selj, jnp.max(s[j], axis=0, keepdims=True) + cj, NEG))
                shifts = [jnp.where(selj, cj - m, NEG) for selj, cj in sel_off] + [-m]
                acc = None
                for j in range(i + 1):
                    d = weighted_values(hh, j, s[j], shifts[j])
                    acc = d if acc is None else acc + d
                finish(hh, i, acc)


SINGLE_PASS_MAX_GAP = 64.0


def _moba(q_t, k_t, v_t, gq, gk):
    B, width, L = q_t.shape
    n_heads = width // HEAD_DIM
    assert n_heads % 2 == 0 and L % MOBA_BLOCK == 0 and MOBA_BLOCK & (MOBA_BLOCK - 1) == 0
    slope2 = jnp.exp2(-8.0 * (jnp.arange(n_heads, dtype=f32) + 1.0) / n_heads) * LOG2E
    hi = slope2.astype(bf16).astype(f32)
    mid = (slope2 - hi).astype(bf16).astype(f32)
    lo = (slope2 - hi - mid).astype(bf16).astype(f32)
    pieces = jnp.stack([hi, mid, lo], axis=1)
    spec = pl.BlockSpec((1, 2 * HEAD_DIM, L), lambda b, h: (b, h, 0))

    def call(single_pass):
        return pl.pallas_call(
            functools.partial(_moba_kernel, single_pass=single_pass),
            grid=(B, n_heads // 2),
            in_specs=[pl.BlockSpec(memory_space=pltpu.SMEM), spec, spec, spec],
            out_specs=spec,
            out_shape=jax.ShapeDtypeStruct((B, width, L), bf16),
            scratch_shapes=[
                pltpu.VMEM((2, L, 2 * HEAD_DIM), bf16),
                pltpu.VMEM((2, V_AUG_ROWS, L), bf16),
                pltpu.VMEM((MOBA_PIPE_DEPTH + 1, MOBA_BLOCK, MOBA_BLOCK), f32),
            ],
            compiler_params=_params(("arbitrary", "arbitrary")),
            name="moba_single_pass" if single_pass else "moba_two_pass",
        )(pieces, q_t, k_t, v_t)

    gap = 2.0 * (HEAD_DIM ** -0.5 * LOG2E) * HEAD_DIM * jnp.max(jnp.abs(gq)) * jnp.max(jnp.abs(gk))
    return lax.cond(gap <= SINGLE_PASS_MAX_GAP, lambda: call(True), lambda: call(False))


SSM_ROW_PITCH = 24


def _gelu_tanh(x):
    c = math.sqrt(2.0 / math.pi)
    w = x * ((-2.0 * c * 0.044715 * LOG2E) * (x * x) + (-2.0 * c * LOG2E))
    return x / (1.0 + jnp.exp2(w))


def _ssm_kernel(u_ref, par_ref, dexp_ref, y_ref, lhs_scr, m_scr, ws_scr, wo_scr, sp_scr, flat_scr):
    B = u_ref.shape[0]
    L = u_ref.shape[2]
    T = SSM_CHUNK
    P = SSM_STATE
    Hc = SSM_GROUP
    nC = L // T
    pitch = SSM_ROW_PITCH

    for c in range(nC):
        for b in range(B):
            flat_scr[(c * B + b) * pitch:(c * B + b) * pitch + Hc, :] = u_ref[b, :, c * T:(c + 1) * T]
    for c in range(nC):
        for hp in range(Hc):
            lhs_scr[c * B:(c + 1) * B, hp * T:(hp + 1) * T] = flat_scr[pl.ds(c * B * pitch + hp, B, stride=pitch), :].astype(bf16)

    P2 = 2 * P
    lr_r, li_r = par_ref[0, 0:1, 0:P], par_ref[0, 1:2, 0:P]
    dt = jnp.exp(par_ref[0, 2:3, 0:1])
    bt_re, bt_im = par_ref[0, 8:8 + Hc, 0:P], par_ref[0, 8:8 + Hc, P:P2]
    c_re, c_im = par_ref[0, 8 + Hc:8 + 2 * Hc, 0:P], par_ref[0, 8 + Hc:8 + 2 * Hc, P:P2]
    col0 = 8 + 2 * Hc
    lr_c, li_c = par_ref[0, col0:col0 + P, 0:1], par_ref[0, col0:col0 + P, 1:2]
    ct_re, ct_im = par_ref[0, col0:col0 + P, Hc:2 * Hc], par_ref[0, col0:col0 + P, 2 * Hc:3 * Hc]

    def lam_pow(lr, li, n):
        mag = jnp.exp(n * (lr * dt))
        ang = n * (li * dt)
        return mag * jnp.cos(ang), mag * jnp.sin(ang)

    ab_re, ab_im = lam_pow(lr_r, li_r, 1.0)
    den = lr_r * lr_r + li_r * li_r
    nr = ab_re - 1.0
    f_re = (nr * lr_r + ab_im * li_r) / den
    f_im = (ab_im * lr_r - nr * li_r) / den
    bb_re = f_re * bt_re - f_im * bt_im
    bb_im = f_re * bt_im + f_im * bt_re

    d_lane = lax.broadcasted_iota(jnp.int32, (P, T), 1).astype(f32)
    pd_re, pd_im = lam_pow(lr_c, li_c, d_lane)

    cb_re = (c_re[:, None, :] * bb_re[None, :, :] - c_im[:, None, :] * bb_im[None, :, :]).reshape(Hc * Hc, P)
    cb_im = (c_re[:, None, :] * bb_im[None, :, :] + c_im[:, None, :] * bb_re[None, :, :]).reshape(Hc * Hc, P)
    kt = (jnp.dot(cb_re, pd_re, precision=HIGHEST, preferred_element_type=f32)
          - jnp.dot(cb_im, pd_im, precision=HIGHEST, preferred_element_type=f32))

    flip = jnp.where(lax.broadcasted_iota(jnp.int32, (T, T), 0) + lax.broadcasted_iota(jnp.int32, (T, T), 1) == T - 1,
                     1.0, 0.0)
    rev_t = lambda a: lax.dot_general(flip, a, (((1,), (1,)), ((), ())), precision=HIGHEST, preferred_element_type=f32)
    pe_re, pe_im = rev_t(pd_re), rev_t(pd_im)
    for hp in range(Hc):
        br, bi = bb_re[hp:hp + 1, :], bb_im[hp:hp + 1, :]
        ws_scr[hp * T:(hp + 1) * T, :] = jnp.concatenate(
            [pe_re * br - pe_im * bi, pe_re * bi + pe_im * br], axis=1).astype(bf16)
    l1_re, l1_im = pd_re[:, 1:2], pd_im[:, 1:2]
    po_re, po_im = pd_re * l1_re - pd_im * l1_im, pd_re * l1_im + pd_im * l1_re
    for h in range(Hc):
        cre, cim = ct_re[:, h:h + 1], ct_im[:, h:h + 1]
        wo_scr[:, h * T:(h + 1) * T] = jnp.concatenate(
            [cre * po_re - cim * po_im, -(cre * po_im + cim * po_re)], axis=0).astype(bf16)

    v = jnp.dot(lhs_scr[...], ws_scr[...], preferred_element_type=f32)
    a_re, a_im = lam_pow(lr_r, li_r, float(T))
    a1 = jnp.concatenate([a_re, a_re], axis=1)
    a2 = jnp.concatenate([-a_im, a_im], axis=1)
    v_sw = pltpu.roll(v, P, 1)
    s = jnp.zeros((B, 2 * P), f32)
    s_sw = jnp.zeros((B, 2 * P), f32)
    for c in range(nC):
        sp_scr[c * B:(c + 1) * B, :] = s.astype(bf16)
        s, s_sw = (a1 * s + a2 * s_sw + v[c * B:(c + 1) * B, :],
                   a1 * s_sw - a2 * s + v_sw[c * B:(c + 1) * B, :])

    tau_i = lax.broadcasted_iota(jnp.int32, (T, T), 0)
    t_i = lax.broadcasted_iota(jnp.int32, (T, T), 1)
    causal = t_i >= tau_i

    for h0 in range(0, Hc, 2):
        cols = slice(h0 * T, (h0 + 2) * T)
        for h in (h0, h0 + 1):
            for hp in range(Hc):
                row = kt[h * Hc + hp:h * Hc + hp + 1, :]
                tile = pltpu.roll(jnp.broadcast_to(row, (T, T)), 0, 1, stride=1, stride_axis=0)
                m_scr[hp * T:(hp + 1) * T, h * T:(h + 1) * T] = jnp.where(causal, tile, 0.0).astype(bf16)
        lhs2 = lhs_scr[:, cols]
        y = jnp.dot(lhs_scr[...], m_scr[:, cols], preferred_element_type=f32)
        y = y + jnp.dot(sp_scr[...], wo_scr[:, cols], preferred_element_type=f32)
        y = _gelu_tanh(y + dexp_ref[0, :, cols] * lhs2.astype(f32))
        for c in range(nC):
            for k, h in enumerate((h0, h0 + 1)):
                flat_scr[pl.ds(c * B * pitch + h, B, stride=pitch), :] = y[c * B:(c + 1) * B, k * T:(k + 1) * T]
    for c in range(nC):
        for b in range(B):
            y_ref[b, :, c * T:(c + 1) * T] = flat_scr[(c * B + b) * pitch:(c * B + b) * pitch + Hc, :].astype(y_ref.dtype)


def _ssm_pack(lambda_re, lambda_im, log_dt, b_re, b_im, c_re, c_im):
    G, P = lambda_re.shape
    Hc = SSM_GROUP
    assert 2 * P == 128 and 3 * Hc <= 128
    lanes = lambda a: jnp.pad(a, ((0, 0), (0, 0), (0, 128 - a.shape[2])))
    head = jnp.pad(jnp.stack([lanes(lambda_re[:, None, :])[:, 0], lanes(lambda_im[:, None, :])[:, 0],
                              lanes(log_dt[:, None, None])[:, 0]], axis=1), ((0, 0), (0, 5), (0, 0)))
    b_t = jnp.concatenate([jnp.swapaxes(b_re, 1, 2), jnp.swapaxes(b_im, 1, 2)], axis=2)
    c_n = jnp.concatenate([c_re, c_im], axis=2)
    cols = jnp.concatenate([lambda_re[:, :, None], lambda_im[:, :, None], jnp.zeros((G, P, Hc - 2), f32),
                            jnp.swapaxes(c_re, 1, 2), jnp.swapaxes(c_im, 1, 2)], axis=2)
    return jnp.concatenate([head, b_t, c_n, lanes(cols)], axis=1)


def _ssm(u_t, lambda_re, lambda_im, log_dt, b_re, b_im, c_re, c_im, d_skip):
    B, width, L = u_t.shape
    G, P = lambda_re.shape
    Hc, T = SSM_GROUP, SSM_CHUNK
    assert width == G * Hc and P == SSM_STATE and L % T == 0
    nC = L // T
    params = _ssm_pack(lambda_re, lambda_im, log_dt, b_re, b_im, c_re, c_im)
    d_exp = jnp.repeat(d_skip.reshape(G, 1, Hc), T, axis=2)
    g3 = lambda shp: pl.BlockSpec((1,) + shp, lambda g: (g, 0, 0))
    u_spec = pl.BlockSpec((B, Hc, L), lambda g: (0, g, 0))
    return pl.pallas_call(
        _ssm_kernel,
        grid=(G,),
        in_specs=[u_spec, g3(params.shape[1:]), g3((1, Hc * T))],
        out_specs=u_spec,
        out_shape=jax.ShapeDtypeStruct((B, width, L), bf16),
        scratch_shapes=[
            pltpu.VMEM((nC * B, Hc * T), bf16),
            pltpu.VMEM((Hc * T, Hc * T), bf16),
            pltpu.VMEM((Hc * T, 2 * P), bf16),
            pltpu.VMEM((2 * P, Hc * T), bf16),
            pltpu.VMEM((nC * B, 2 * P), bf16),
            pltpu.VMEM((nC * B * SSM_ROW_PITCH, T), f32),
        ],
        compiler_params=_params(("arbitrary",)),
        name="ssm",
    )(u_t, params, d_exp)


MIX_CHUNK = 256


def _mix_kernel(x_ref, a_ref, y_ref, wg_ref, bg_ref, ga_ref, gs_ref, wo_ref, o_ref):
    def rms_rows(t, g_ref):
        return t * lax.rsqrt(jnp.mean(t * t, axis=0, keepdims=True) + EPS) * g_ref[...]

    for c in range(0, x_ref.shape[1], MIX_CHUNK):
        tok = slice(c, c + MIX_CHUNK)
        a_n = rms_rows(a_ref[0, :, tok].astype(f32), ga_ref)
        yg = y_ref[0, :, tok]
        z = jnp.dot(wg_ref[...], yg, preferred_element_type=f32) + bg_ref[...]
        s_n = rms_rows(yg.astype(f32) * jax.nn.sigmoid(z), gs_ref)
        mixed_t = jnp.concatenate([a_n, s_n], axis=0).astype(bf16)
        delta = lax.dot_general(mixed_t, wo_ref[...], (((0,), (0,)), ((), ())), preferred_element_type=f32)
        o_ref[0, tok, :] = x_ref[0, tok, :] + delta


def _mix(x, attn_t, yg_t, w_glu_t, b_glu, g_attn, g_ssm, w_out, tn=1024):
    B, L, D = x.shape
    aw, sw = attn_t.shape[1], yg_t.shape[1]
    assert aw + sw == D and L % tn == 0 and tn % MIX_CHUNK == 0
    const = lambda shp: pl.BlockSpec(shp, lambda b, i: (0, 0), pipeline_mode=pl.Buffered(1))
    return pl.pallas_call(
        _mix_kernel,
        grid=(B, L // tn),
        in_specs=[
            pl.BlockSpec((1, tn, D), lambda b, i: (b, i, 0)),
            pl.BlockSpec((1, aw, tn), lambda b, i: (b, 0, i)),
            pl.BlockSpec((1, sw, tn), lambda b, i: (b, 0, i)),
            const((sw, sw)), const((sw, 1)), const((aw, 1)), const((sw, 1)), const((D, D)),
        ],
        out_specs=pl.BlockSpec((1, tn, D), lambda b, i: (b, i, 0)),
        out_shape=jax.ShapeDtypeStruct((B, L, D), f32),
        compiler_params=_params(("arbitrary", "arbitrary")),
        name="mix",
    )(x, attn_t, yg_t, w_glu_t, b_glu, g_attn, g_ssm, w_out)


def _ffn_kernel(x_ref, g2_ref, w1_ref, w2_ref, o_ref, h_scr):
    f = pl.program_id(1)

    @pl.when(f == 0)
    def _():
        x = x_ref[...]
        ms = jnp.mean(x * x, axis=-1, keepdims=True)
        h_scr[...] = (x * lax.rsqrt(ms + EPS) * g2_ref[...]).astype(bf16)
        o_ref[...] = x

    a = jnp.dot(h_scr[...], w1_ref[...], preferred_element_type=f32)
    a = jnp.square(jnp.maximum(a, 0.0)).astype(bf16)
    o_ref[...] += jnp.dot(a, w2_ref[...], preferred_element_type=f32)


def _ffn(x2d, g2, w1, w2, tm=512, tf=1024):
    N, D = x2d.shape
    F = w1.shape[1]
    assert N % tm == 0 and F % tf == 0
    return pl.pallas_call(
        _ffn_kernel,
        grid=(N // tm, F // tf),
        in_specs=[
            pl.BlockSpec((tm, D), lambda i, f: (i, 0)),
            pl.BlockSpec((1, D), lambda i, f: (0, 0)),
            pl.BlockSpec((D, tf), lambda i, f: (0, f)),
            pl.BlockSpec((tf, D), lambda i, f: (f, 0)),
        ],
        out_specs=pl.BlockSpec((tm, D), lambda i, f: (i, 0)),
        out_shape=jax.ShapeDtypeStruct((N, D), f32),
        scratch_shapes=[pltpu.VMEM((tm, D), bf16)],
        compiler_params=_params(("arbitrary", "arbitrary")),
        name="ffn",
    )(x2d, g2, w1, w2)


def _layer(x, norm1_gain, w_in, q_norm_gain, k_norm_gain, attn_out_gain, lambda_re, lambda_im, log_dt,
           b_re, b_im, c_re, c_im, d_skip, w_glu, b_glu, ssm_out_gain, w_out, norm2_gain, w_ff1, w_ff2):
    B, L, D = x.shape
    sw = d_skip.shape[0]
    aw = D - sw
    assert aw == sw and w_in.shape == (D, 3 * aw + sw)
    q_t, k_t, v_t, u_t = _in_proj(
        x, norm1_gain.reshape(1, D), w_in.astype(bf16),
        q_norm_gain.reshape(HEAD_DIM, 1), k_norm_gain.reshape(HEAD_DIM, 1), aw)
    attn_t = _moba(q_t, k_t, v_t, q_norm_gain, k_norm_gain)
    yg_t = _ssm(u_t, lambda_re, lambda_im, log_dt, b_re, b_im, c_re, c_im, d_skip)
    x1 = _mix(x, attn_t, yg_t, w_glu.T.astype(bf16), b_glu.reshape(sw, 1),
              attn_out_gain.reshape(aw, 1), ssm_out_gain.reshape(sw, 1), w_out.astype(bf16))
    out = _ffn(x1.reshape(B * L, D), norm2_gain.reshape(1, D), w_ff1.astype(bf16), w_ff2.astype(bf16))
    return out.reshape(B, L, D)


def kernel(x, norm1_gain, w_in, q_norm_gain, k_norm_gain, attn_out_gain, lambda_re, lambda_im, log_dt, b_re, b_im, c_re, c_im, d_skip, w_glu, b_glu, ssm_out_gain, w_out, norm2_gain, w_ff1, w_ff2):
    per_layer = (norm1_gain, w_in, q_norm_gain, k_norm_gain, attn_out_gain, lambda_re, lambda_im, log_dt,
                 b_re, b_im, c_re, c_im, d_skip, w_glu, b_glu, ssm_out_gain, w_out, norm2_gain, w_ff1, w_ff2)
    for i in range(w_in.shape[0]):
        x = _layer(x, *(p[i] for p in per_layer))
    return x
```

```python
import functools
import math

import jax
import jax.numpy as jnp
from jax import lax
from jax.experimental import pallas as pl
from jax.experimental.pallas import tpu as pltpu

f32 = jnp.float32
bf16 = jnp.bfloat16

EPS = 1e-6
NEG = -1e30
HEAD_DIM = 64
MOBA_BLOCK = 256
MOBA_TOPK = 3
SSM_GROUP = 16
SSM_STATE = 64
SSM_CHUNK = 128
VMEM_LIMIT = 56 * 1024 * 1024
HIGHEST = lax.Precision.HIGHEST


def _params(sem):
    return pltpu.CompilerParams(dimension_semantics=sem, vmem_limit_bytes=VMEM_LIMIT)


def _in_proj_kernel(x_ref, g1_ref, w_ref, gq_ref, gk_ref, q_ref, k_ref, v_ref, u_ref, h_scr):
    j = pl.program_id(1)
    i = pl.program_id(2)
    tm = x_ref.shape[1]

    @pl.when(j == 0)
    def _():
        x = x_ref[0]
        ms = jnp.mean(x * x, axis=-1, keepdims=True)
        h_scr[i] = (x * lax.rsqrt(ms + EPS) * g1_ref[...]).astype(bf16)

    def project(out_ref, finish):
        t = jnp.dot(h_scr[i], w_ref[...], preferred_element_type=f32).T
        out_ref[0] = finish(t)

    def head_norm(g_ref):
        def finish(t):
            t3 = t.reshape(t.shape[0] // HEAD_DIM, HEAD_DIM, tm)
            ms = jnp.mean(t3 * t3, axis=1, keepdims=True)
            return (t3 * lax.rsqrt(ms + EPS) * g_ref[...][None]).reshape(t.shape)
        return finish

    @pl.when(j == 0)
    def _():
        project(q_ref, head_norm(gq_ref))

    @pl.when(j == 1)
    def _():
        project(k_ref, head_norm(gk_ref))

    @pl.when(j == 2)
    def _():
        project(v_ref, lambda t: t.astype(bf16))

    @pl.when(j == 3)
    def _():
        project(u_ref, lambda t: t)


def _in_proj(x, g1, w_in, gq, gk, width, tm=512):
    B, L, D = x.shape
    assert w_in.shape == (D, 4 * width) and L % tm == 0
    n_i = L // tm
    last = n_i - 1

    def out_spec(jj):
        return pl.BlockSpec((1, width, tm),
                            lambda b, j, i: (b, 0, jnp.where(j == jj, i, jnp.where(j < jj, 0, last))))

    return pl.pallas_call(
        _in_proj_kernel,
        grid=(B, 4, n_i),
        in_specs=[
            pl.BlockSpec((1, tm, D), lambda b, j, i: (b, jnp.where(j == 0, i, last), 0)),
            pl.BlockSpec((1, D), lambda b, j, i: (0, 0)),
            pl.BlockSpec((D, width), lambda b, j, i: (0, j)),
            pl.BlockSpec((HEAD_DIM, 1), lambda b, j, i: (0, 0)),
            pl.BlockSpec((HEAD_DIM, 1), lambda b, j, i: (0, 0)),
        ],
        out_specs=[out_spec(0), out_spec(1), out_spec(2), out_spec(3)],
        out_shape=[
            jax.ShapeDtypeStruct((B, width, L), f32),
            jax.ShapeDtypeStruct((B, width, L), f32),
            jax.ShapeDtypeStruct((B, width, L), bf16),
            jax.ShapeDtypeStruct((B, width, L), f32),
        ],
        scratch_shapes=[pltpu.VMEM((n_i, tm, D), bf16)],
        compiler_params=_params(("arbitrary", "arbitrary", "arbitrary")),
        name="in_proj",
    )(x, g1, w_in, gq, gk)


LOG2E = 1.4426950408889634
V_AUG_ROWS = HEAD_DIM + 16
MOBA_PIPE_DEPTH = 5


def _moba_kernel(sp_ref, q_ref, k_ref, v_ref, o_ref, ktok_scr, vaug_scr, s_scr, *, single_pass):
    L = q_ref.shape[2]
    blk = MOBA_BLOCK
    nb = L // blk
    top_k = min(MOBA_TOPK, nb - 1)
    qscale = HEAD_DIM ** -0.5 * LOG2E
    pair = pl.program_id(1)

    r_i = lax.broadcasted_iota(jnp.int32, (blk, blk), 0)
    c_i = lax.broadcasted_iota(jnp.int32, (blk, blk), 1)
    causal = r_i <= c_i
    blk_id = lax.broadcasted_iota(jnp.int32, (nb, blk), 0)
    q_row = lax.broadcasted_iota(jnp.int32, (HEAD_DIM, blk), 0)
    q_off = lax.broadcasted_iota(jnp.int32, (1, blk), 1).astype(f32)

    k_t = k_ref[0]
    ktok = k_t.T
    kmean = jnp.concatenate(
        [jnp.mean(ktok[j * blk:(j + 1) * blk], axis=0, keepdims=True) for j in range(nb)], axis=0)
    lane = lax.broadcasted_iota(jnp.int32, ktok.shape, 1)
    key_off = jnp.bitwise_and(lax.broadcasted_iota(jnp.int32, ktok.shape, 0), blk - 1).astype(f32)
    ones_rows = jnp.where(lax.broadcasted_iota(jnp.int32, (V_AUG_ROWS - HEAD_DIM, L), 0) == 0, 1.0, 0.0).astype(bf16)
    for hh in range(2):
        kh = ktok if hh == 0 else pltpu.roll(ktok, HEAD_DIM, 1)
        ktok_scr[hh] = jnp.where(lane < HEAD_DIM, kh, jnp.where(lane < HEAD_DIM + 3, key_off, 0.0)).astype(bf16)
        vaug_scr[hh, 0:HEAD_DIM, :] = v_ref[0, hh * HEAD_DIM:(hh + 1) * HEAD_DIM, :]
        vaug_scr[hh, HEAD_DIM:, :] = ones_rows

    @functools.cache
    def head_setup(hh):
        rows = slice(hh * HEAD_DIM, (hh + 1) * HEAD_DIM)
        head = pair * 2 + hh
        pieces = [jnp.full((HEAD_DIM, blk), sp_ref[head, t], f32) for t in range(3)]
        q_extra = jnp.where(q_row == 0, pieces[0], jnp.where(q_row == 1, pieces[1],
                            jnp.where(q_row == 2, pieces[2], 0.0))).astype(bf16)
        slope2 = sum(jnp.full((1, blk), sp_ref[head, t], f32) for t in range(3))
        kmax = []
        if single_pass:
            kh_t = k_t[rows, :]
            kn2 = jnp.sum(kh_t * kh_t, axis=0, keepdims=True)
            for j in range(nb):
                bj = jnp.sqrt(jnp.max(kn2[:, j * blk:(j + 1) * blk], axis=1, keepdims=True))
                kmax.append(bj if j == 0 else jnp.maximum(kmax[-1], bj))
        return rows, q_extra, slope2, kmax

    @functools.cache
    def block_setup(hh, i):
        rows, q_extra, slope2, kmax = head_setup(hh)
        qf = q_ref[0, rows, i * blk:(i + 1) * blk]
        q_aug = jnp.concatenate([(qf * qscale).astype(bf16), q_extra], axis=0)

        def scores(j):
            s = jnp.dot(ktok_scr[hh, j * blk:(j + 1) * blk, :], q_aug, preferred_element_type=f32)
            return jnp.where(causal, s, NEG) if j == i else s

        sel_off = []
        if i > 0:
            z = jnp.zeros_like(qf)
            q_pair = jnp.concatenate([qf, z] if hh == 0 else [z, qf], axis=0)
            gate = jnp.dot(kmean, q_pair, precision=HIGHEST, preferred_element_type=f32)
            rank = jnp.zeros((nb, blk), f32)
            for jp in range(i):
                row = gate[jp:jp + 1, :]
                beats = (row > gate) | ((row == gate) & (jp < blk_id))
                rank = rank + jnp.where(beats, 1.0, 0.0)
            sel = jnp.where((blk_id < i) & (rank < top_k), 1.0, 0.0)
            sel_off = [(sel[j:j + 1, :] > 0.5, slope2 * float((j - i) * blk)) for j in range(i)]
        shifts = None
        if single_pass:
            q_norm = jnp.sqrt(jnp.sum(qf * qf, axis=0, keepdims=True))
            m = (qscale * q_norm) * kmax[i] + slope2 * q_off
            shifts = [jnp.where(selj, cj - m, NEG) for selj, cj in sel_off] + [-m]
        return scores, sel_off, shifts

    def finish(hh, i, acc):
        rows = head_setup(hh)[0]
        o_ref[0, rows, i * blk:(i + 1) * blk] = (acc[0:HEAD_DIM] / acc[HEAD_DIM:HEAD_DIM + 1]).astype(o_ref.dtype)

    def weighted_values(hh, j, s, shift):
        p = jnp.exp2(s + shift).astype(bf16)
        return jnp.dot(vaug_scr[hh, :, j * blk:(j + 1) * blk], p, preferred_element_type=f32)

    if single_pass:
        order = [(hh, i, j) for hh in range(2) for i in range(nb) for j in range(i + 1)]
        depth = MOBA_PIPE_DEPTH

        def issue_scores(n):
            hh, i, j = order[n]
            s_scr[n % (depth + 1)] = block_setup(hh, i)[0](j)

        for n in range(depth):
            issue_scores(n)
        acc = None
        for n, (hh, i, j) in enumerate(order):
            if n + depth < len(order):
                issue_scores(n + depth)
            d = weighted_values(hh, j, s_scr[n % (depth + 1)], block_setup(hh, i)[2][j])
            acc = d if j == 0 else acc + d
            if j == i:
                finish(hh, i, acc)
    else:
        for hh in range(2):
            for i in range(nb):
                scores, sel_off, _ = block_setup(hh, i)
                s = [scores(j) for j in range(i + 1)]
                m = jnp.max(s[i], axis=0, keepdims=True)
                for j in range(i):
                    selj, cj = sel_off[j]
                    m = jnp.maximum(m, jnp.where(selj, jnp.max(s[j], axis=0, keepdims=True) + cj, NEG))
                shifts = [jnp.where(selj, cj - m, NEG) for selj, cj in sel_off] + [-m]
                acc = None
                for j in range(i + 1):
                    d = weighted_values(hh, j, s[j], shifts[j])
                    acc = d if acc is None else acc + d
                finish(hh, i, acc)


SINGLE_PASS_MAX_GAP = 64.0


def _moba(q_t, k_t, v_t, gq, gk):
    B, width, L = q_t.shape
    n_heads = width // HEAD_DIM
    assert n_heads % 2 == 0 and L % MOBA_BLOCK == 0 and MOBA_BLOCK & (MOBA_BLOCK - 1) == 0
    slope2 = jnp.exp2(-8.0 * (jnp.arange(n_heads, dtype=f32) + 1.0) / n_heads) * LOG2E
    hi = slope2.astype(bf16).astype(f32)
    mid = (slope2 - hi).astype(bf16).astype(f32)
    lo = (slope2 - hi - mid).astype(bf16).astype(f32)
    pieces = jnp.stack([hi, mid, lo], axis=1)
    spec = pl.BlockSpec((1, 2 * HEAD_DIM, L), lambda b, h: (b, h, 0))

    def call(single_pass):
        return pl.pallas_call(
            functools.partial(_moba_kernel, single_pass=single_pass),
            grid=(B, n_heads // 2),
            in_specs=[pl.BlockSpec(memory_space=pltpu.SMEM), spec, spec, spec],
            out_specs=spec,
            out_shape=jax.ShapeDtypeStruct((B, width, L), bf16),
            scratch_shapes=[
                pltpu.VMEM((2, L, 2 * HEAD_DIM), bf16),
                pltpu.VMEM((2, V_AUG_ROWS, L), bf16),
                pltpu.VMEM((MOBA_PIPE_DEPTH + 1, MOBA_BLOCK, MOBA_BLOCK), f32),
            ],
            compiler_params=_params(("arbitrary", "arbitrary")),
            name="moba_single_pass" if single_pass else "moba_two_pass",
        )(pieces, q_t, k_t, v_t)

    gap = 2.0 * (HEAD_DIM ** -0.5 * LOG2E) * HEAD_DIM * jnp.max(jnp.abs(gq)) * jnp.max(jnp.abs(gk))
    return lax.cond(gap <= SINGLE_PASS_MAX_GAP, lambda: call(True), lambda: call(False))


SSM_ROW_PITCH = 24


def _gelu_tanh(x):
    c = math.sqrt(2.0 / math.pi)
    w = x * ((-2.0 * c * 0.044715 * LOG2E) * (x * x) + (-2.0 * c * LOG2E))
    return x / (1.0 + jnp.exp2(w))


def _ssm_kernel(u_ref, par_ref, dexp_ref, y_ref, lhs_scr, m_scr, ws_scr, wo_scr, sp_scr, flat_scr):
    B = u_ref.shape[0]
    L = u_ref.shape[2]
    T = SSM_CHUNK
    P = SSM_STATE
    Hc = SSM_GROUP
    nC = L // T
    pitch = SSM_ROW_PITCH

    for c in range(nC):
        for b in range(B):
            flat_scr[(c * B + b) * pitch:(c * B + b) * pitch + Hc, :] = u_ref[b, :, c * T:(c + 1) * T]
    for c in range(nC):
        for hp in range(Hc):
            lhs_scr[c * B:(c + 1) * B, hp * T:(hp + 1) * T] = flat_scr[pl.ds(c * B * pitch + hp, B, stride=pitch), :].astype(bf16)

    P2 = 2 * P
    lr_r, li_r = par_ref[0, 0:1, 0:P], par_ref[0, 1:2, 0:P]
    dt = jnp.exp(par_ref[0, 2:3, 0:1])
    bt_re, bt_im = par_ref[0, 8:8 + Hc, 0:P], par_ref[0, 8:8 + Hc, P:P2]
    c_re, c_im = par_ref[0, 8 + Hc:8 + 2 * Hc, 0:P], par_ref[0, 8 + Hc:8 + 2 * Hc, P:P2]
    col0 = 8 + 2 * Hc
    lr_c, li_c = par_ref[0, col0:col0 + P, 0:1], par_ref[0, col0:col0 + P, 1:2]
    ct_re, ct_im = par_ref[0, col0:col0 + P, Hc:2 * Hc], par_ref[0, col0:col0 + P, 2 * Hc:3 * Hc]

    def lam_pow(lr, li, n):
        mag = jnp.exp(n * (lr * dt))
        ang = n * (li * dt)
        return mag * jnp.cos(ang), mag * jnp.sin(ang)

    ab_re, ab_im = lam_pow(lr_r, li_r, 1.0)
    den = lr_r * lr_r + li_r * li_r
    nr = ab_re - 1.0
    f_re = (nr * lr_r + ab_im * li_r) / den
    f_im = (ab_im * lr_r - nr * li_r) / den
    bb_re = f_re * bt_re - f_im * bt_im
    bb_im = f_re * bt_im + f_im * bt_re

    d_lane = lax.broadcasted_iota(jnp.int32, (P, T), 1).astype(f32)
    pd_re, pd_im = lam_pow(lr_c, li_c, d_lane)

    cb_re = (c_re[:, None, :] * bb_re[None, :, :] - c_im[:, None, :] * bb_im[None, :, :]).reshape(Hc * Hc, P)
    cb_im = (c_re[:, None, :] * bb_im[None, :, :] + c_im[:, None, :] * bb_re[None, :, :]).reshape(Hc * Hc, P)
    kt = (jnp.dot(cb_re, pd_re, precision=HIGHEST, preferred_element_type=f32)
          - jnp.dot(cb_im, pd_im, precision=HIGHEST, preferred_element_type=f32))

    flip = jnp.where(lax.broadcasted_iota(jnp.int32, (T, T), 0) + lax.broadcasted_iota(jnp.int32, (T, T), 1) == T - 1,
                     1.0, 0.0)
    rev_t = lambda a: lax.dot_general(flip, a, (((1,), (1,)), ((), ())), precision=HIGHEST, preferred_element_type=f32)
    pe_re, pe_im = rev_t(pd_re), rev_t(pd_im)
    for hp in range(Hc):
        br, bi = bb_re[hp:hp + 1, :], bb_im[hp:hp + 1, :]
        ws_scr[hp * T:(hp + 1) * T, :] = jnp.concatenate(
            [pe_re * br - pe_im * bi, pe_re * bi + pe_im * br], axis=1).astype(bf16)
    l1_re, l1_im = pd_re[:, 1:2], pd_im[:, 1:2]
    po_re, po_im = pd_re * l1_re - pd_im * l1_im, pd_re * l1_im + pd_im * l1_re
    for h in range(Hc):
        cre, cim = ct_re[:, h:h + 1], ct_im[:, h:h + 1]
        wo_scr[:, h * T:(h + 1) * T] = jnp.concatenate(
            [cre * po_re - cim * po_im, -(cre * po_im + cim * po_re)], axis=0).astype(bf16)

    v = jnp.dot(lhs_scr[...], ws_scr[...], preferred_element_type=f32)
    a_re, a_im = lam_pow(lr_r, li_r, float(T))
    a1 = jnp.concatenate([a_re, a_re], axis=1)
    a2 = jnp.concatenate([-a_im, a_im], axis=1)
    v_sw = pltpu.roll(v, P, 1)
    s = jnp.zeros((B, 2 * P), f32)
    s_sw = jnp.zeros((B, 2 * P), f32)
    for c in range(nC):
        sp_scr[c * B:(c + 1) * B, :] = s.astype(bf16)
        s, s_sw = (a1 * s + a2 * s_sw + v[c * B:(c + 1) * B, :],
                   a1 * s_sw - a2 * s + v_sw[c * B:(c + 1) * B, :])

    tau_i = lax.broadcasted_iota(jnp.int32, (T, T), 0)
    t_i = lax.broadcasted_iota(jnp.int32, (T, T), 1)
    causal = t_i >= tau_i

    for h0 in range(0, Hc, 2):
        cols = slice(h0 * T, (h0 + 2) * T)
        for h in (h0, h0 + 1):
            for hp in range(Hc):
                row = kt[h * Hc + hp:h * Hc + hp + 1, :]
                tile = pltpu.roll(jnp.broadcast_to(row, (T, T)), 0, 1, stride=1, stride_axis=0)
                m_scr[hp * T:(hp + 1) * T, h * T:(h + 1) * T] = jnp.where(causal, tile, 0.0).astype(bf16)
        lhs2 = lhs_scr[:, cols]
        y = jnp.dot(lhs_scr[...], m_scr[:, cols], preferred_element_type=f32)
        y = y + jnp.dot(sp_scr[...], wo_scr[:, cols], preferred_element_type=f32)
        y = _gelu_tanh(y + dexp_ref[0, :, cols] * lhs2.astype(f32))
        for c in range(nC):
            for k, h in enumerate((h0, h0 + 1)):
                flat_scr[pl.ds(c * B * pitch + h, B, stride=pitch), :] = y[c * B:(c + 1) * B, k * T:(k + 1) * T]
    for c in range(nC):
        for b in range(B):
            y_ref[b, :, c * T:(c + 1) * T] = flat_scr[(c * B + b) * pitch:(c * B + b) * pitch + Hc, :].astype(y_ref.dtype)


def _ssm_pack(lambda_re, lambda_im, log_dt, b_re, b_im, c_re, c_im):
    G, P = lambda_re.shape
    Hc = SSM_GROUP
    assert 2 * P == 128 and 3 * Hc <= 128
    lanes = lambda a: jnp.pad(a, ((0, 0), (0, 0), (0, 128 - a.shape[2])))
    head = jnp.pad(jnp.stack([lanes(lambda_re[:, None, :])[:, 0], lanes(lambda_im[:, None, :])[:, 0],
                              lanes(log_dt[:, None, None])[:, 0]], axis=1), ((0, 0), (0, 5), (0, 0)))
    b_t = jnp.concatenate([jnp.swapaxes(b_re, 1, 2), jnp.swapaxes(b_im, 1, 2)], axis=2)
    c_n = jnp.concatenate([c_re, c_im], axis=2)
    cols = jnp.concatenate([lambda_re[:, :, None], lambda_im[:, :, None], jnp.zeros((G, P, Hc - 2), f32),
                            jnp.swapaxes(c_re, 1, 2), jnp.swapaxes(c_im, 1, 2)], axis=2)
    return jnp.concatenate([head, b_t, c_n, lanes(cols)], axis=1)


def _ssm(u_t, lambda_re, lambda_im, log_dt, b_re, b_im, c_re, c_im, d_skip):
    B, width, L = u_t.shape
    G, P = lambda_re.shape
    Hc, T = SSM_GROUP, SSM_CHUNK
    assert width == G * Hc and P == SSM_STATE and L % T == 0
    nC = L // T
    params = _ssm_pack(lambda_re, lambda_im, log_dt, b_re, b_im, c_re, c_im)
    d_exp = jnp.repeat(d_skip.reshape(G, 1, Hc), T, axis=2)
    g3 = lambda shp: pl.BlockSpec((1,) + shp, lambda g: (g, 0, 0))
    u_spec = pl.BlockSpec((B, Hc, L), lambda g: (0, g, 0))
    return pl.pallas_call(
        _ssm_kernel,
        grid=(G,),
        in_specs=[u_spec, g3(params.shape[1:]), g3((1, Hc * T))],
        out_specs=u_spec,
        out_shape=jax.ShapeDtypeStruct((B, width, L), bf16),
        scratch_shapes=[
            pltpu.VMEM((nC * B, Hc * T), bf16),
            pltpu.VMEM((Hc * T, Hc * T), bf16),
            pltpu.VMEM((Hc * T, 2 * P), bf16),
            pltpu.VMEM((2 * P, Hc * T), bf16),
            pltpu.VMEM((nC * B, 2 * P), bf16),
            pltpu.VMEM((nC * B * SSM_ROW_PITCH, T), f32),
        ],
        compiler_params=_params(("arbitrary",)),
        name="ssm",
    )(u_t, params, d_exp)


MIX_CHUNK = 256


def _mix_kernel(x_ref, a_ref, y_ref, wg_ref, bg_ref, ga_ref, gs_ref, wo_ref, o_ref):
    def rms_rows(t, g_ref):
        return t * lax.rsqrt(jnp.mean(t * t, axis=0, keepdims=True) + EPS) * g_ref[...]

    for c in range(0, x_ref.shape[1], MIX_CHUNK):
        tok = slice(c, c + MIX_CHUNK)
        a_n = rms_rows(a_ref[0, :, tok].astype(f32), ga_ref)
        yg = y_ref[0, :, tok]
        z = jnp.dot(wg_ref[...], yg, preferred_element_type=f32) + bg_ref[...]
        s_n = rms_rows(yg.astype(f32) * jax.nn.sigmoid(z), gs_ref)
        mixed_t = jnp.concatenate([a_n, s_n], axis=0).astype(bf16)
        delta = lax.dot_general(mixed_t, wo_ref[...], (((0,), (0,)), ((), ())), preferred_element_type=f32)
        o_ref[0, tok, :] = x_ref[0, tok, :] + delta


def _mix(x, attn_t, yg_t, w_glu_t, b_glu, g_attn, g_ssm, w_out, tn=1024):
    B, L, D = x.shape
    aw, sw = attn_t.shape[1], yg_t.shape[1]
    assert aw + sw == D and L % tn == 0 and tn % MIX_CHUNK == 0
    const = lambda shp: pl.BlockSpec(shp, lambda b, i: (0, 0), pipeline_mode=pl.Buffered(1))
    return pl.pallas_call(
        _mix_kernel,
        grid=(B, L // tn),
        in_specs=[
            pl.BlockSpec((1, tn, D), lambda b, i: (b, i, 0)),
            pl.BlockSpec((1, aw, tn), lambda b, i: (b, 0, i)),
            pl.BlockSpec((1, sw, tn), lambda b, i: (b, 0, i)),
            const((sw, sw)), const((sw, 1)), const((aw, 1)), const((sw, 1)), const((D, D)),
        ],
        out_specs=pl.BlockSpec((1, tn, D), lambda b, i: (b, i, 0)),
        out_shape=jax.ShapeDtypeStruct((B, L, D), f32),
        compiler_params=_params(("arbitrary", "arbitrary")),
        name="mix",
    )(x, attn_t, yg_t, w_glu_t, b_glu, g_attn, g_ssm, w_out)


def _ffn_kernel(x_ref, g2_ref, w1_ref, w2_ref, o_ref, h_scr):
    f = pl.program_id(1)

    @pl.when(f == 0)
    def _():
        x = x_ref[...]
        ms = jnp.mean(x * x, axis=-1, keepdims=True)
        h_scr[...] = (x * lax.rsqrt(ms + EPS) * g2_ref[...]).astype(bf16)
        o_ref[...] = x

    a = jnp.dot(h_scr[...], w1_ref[...], preferred_element_type=f32)
    a = jnp.square(jnp.maximum(a, 0.0)).astype(bf16)
    o_ref[...] += jnp.dot(a, w2_ref[...], preferred_element_type=f32)


def _ffn(x2d, g2, w1, w2, tm=512, tf=1024):
    N, D = x2d.shape
    F = w1.shape[1]
    assert N % tm == 0 and F % tf == 0
    return pl.pallas_call(
        _ffn_kernel,
        grid=(N // tm, F // tf),
        in_specs=[
            pl.BlockSpec((tm, D), lambda i, f: (i, 0)),
            pl.BlockSpec((1, D), lambda i, f: (0, 0)),
            pl.BlockSpec((D, tf), lambda i, f: (0, f)),
            pl.BlockSpec((tf, D), lambda i, f: (f, 0)),
        ],
        out_specs=pl.BlockSpec((tm, D), lambda i, f: (i, 0)),
        out_shape=jax.ShapeDtypeStruct((N, D), f32),
        scratch_shapes=[pltpu.VMEM((tm, D), bf16)],
        compiler_params=_params(("arbitrary", "arbitrary")),
        name="ffn",
    )(x2d, g2, w1, w2)


def _layer(x, norm1_gain, w_in, q_norm_gain, k_norm_gain, attn_out_gain, lambda_re, lambda_im, log_dt,
           b_re, b_im, c_re, c_im, d_skip, w_glu, b_glu, ssm_out_gain, w_out, norm2_gain, w_ff1, w_ff2):
    B, L, D = x.shape
    sw = d_skip.shape[0]
    aw = D - sw
    assert aw == sw and w_in.shape == (D, 3 * aw + sw)
    q_t, k_t, v_t, u_t = _in_proj(
        x, norm1_gain.reshape(1, D), w_in.astype(bf16),
        q_norm_gain.reshape(HEAD_DIM, 1), k_norm_gain.reshape(HEAD_DIM, 1), aw)
    attn_t = _moba(q_t, k_t, v_t, q_norm_gain, k_norm_gain)
    yg_t = _ssm(u_t, lambda_re, lambda_im, log_dt, b_re, b_im, c_re, c_im, d_skip)
    x1 = _mix(x, attn_t, yg_t, w_glu.T.astype(bf16), b_glu.reshape(sw, 1),
              attn_out_gain.reshape(aw, 1), ssm_out_gain.reshape(sw, 1), w_out.astype(bf16))
    out = _ffn(x1.reshape(B * L, D), norm2_gain.reshape(1, D), w_ff1.astype(bf16), w_ff2.astype(bf16))
    return out.reshape(B, L, D)


def kernel(x, norm1_gain, w_in, q_norm_gain, k_norm_gain, attn_out_gain, lambda_re, lambda_im, log_dt, b_re, b_im, c_re, c_im, d_skip, w_glu, b_glu, ssm_out_gain, w_out, norm2_gain, w_ff1, w_ff2):
    per_layer = (norm1_gain, w_in, q_norm_gain, k_norm_gain, attn_out_gain, lambda_re, lambda_im, log_dt,
                 b_re, b_im, c_re, c_im, d_skip, w_glu, b_glu, ssm_out_gain, w_out, norm2_gain, w_ff1, w_ff2)
    for i in range(w_in.shape[0]):
        x = _layer(x, *(p[i] for p in per_layer))
    return x
```

```python
import functools
import math

import jax
import jax.numpy as jnp
from jax import lax
from jax.experimental import pallas as pl
from jax.experimental.pallas import tpu as pltpu

f32 = jnp.float32
bf16 = jnp.bfloat16

EPS = 1e-6
NEG = -1e30
HEAD_DIM = 64
MOBA_BLOCK = 256
MOBA_TOPK = 3
SSM_GROUP = 16
SSM_STATE = 64
SSM_CHUNK = 64
VMEM_LIMIT = 56 * 1024 * 1024
HIGHEST = lax.Precision.HIGHEST


def _params(sem):
    return pltpu.CompilerParams(dimension_semantics=sem, vmem_limit_bytes=VMEM_LIMIT)


def _in_proj_kernel(x_ref, g1_ref, w_ref, gq_ref, gk_ref, q_ref, k_ref, v_ref, u_ref, h_scr):
    j = pl.program_id(1)
    i = pl.program_id(2)
    tm = x_ref.shape[1]

    @pl.when(j == 0)
    def _():
        x = x_ref[0]
        ms = jnp.mean(x * x, axis=-1, keepdims=True)
        h_scr[i] = (x * lax.rsqrt(ms + EPS) * g1_ref[...]).astype(bf16)

    def project(out_ref, finish):
        t = lax.dot_general(w_ref[...], h_scr[i], (((0,), (1,)), ((), ())), preferred_element_type=f32)
        out_ref[0] = finish(t)

    def head_norm(g_ref):
        def finish(t):
            t3 = t.reshape(t.shape[0] // HEAD_DIM, HEAD_DIM, tm)
            ms = jnp.mean(t3 * t3, axis=1, keepdims=True)
            return (t3 * lax.rsqrt(ms + EPS) * g_ref[...][None]).reshape(t.shape)
        return finish

    @pl.when(j == 0)
    def _():
        project(q_ref, head_norm(gq_ref))

    @pl.when(j == 1)
    def _():
        project(k_ref, head_norm(gk_ref))

    @pl.when(j == 2)
    def _():
        project(v_ref, lambda t: t.astype(bf16))

    @pl.when(j == 3)
    def _():
        project(u_ref, lambda t: t)


def _in_proj(x, g1, w_in, gq, gk, width, tm=512):
    B, L, D = x.shape
    assert w_in.shape == (D, 4 * width) and L % tm == 0
    n_i = L // tm
    last = n_i - 1

    def out_spec(jj):
        return pl.BlockSpec((1, width, tm),
                            lambda b, j, i: (b, 0, jnp.where(j == jj, i, jnp.where(j < jj, 0, last))))

    return pl.pallas_call(
        _in_proj_kernel,
        grid=(B, 4, n_i),
        in_specs=[
            pl.BlockSpec((1, tm, D), lambda b, j, i: (b, jnp.where(j == 0, i, last), 0)),
            pl.BlockSpec((1, D), lambda b, j, i: (0, 0)),
            pl.BlockSpec((D, width), lambda b, j, i: (0, j)),
            pl.BlockSpec((HEAD_DIM, 1), lambda b, j, i: (0, 0)),
            pl.BlockSpec((HEAD_DIM, 1), lambda b, j, i: (0, 0)),
        ],
        out_specs=[out_spec(0), out_spec(1), out_spec(2), out_spec(3)],
        out_shape=[
            jax.ShapeDtypeStruct((B, width, L), f32),
            jax.ShapeDtypeStruct((B, width, L), f32),
            jax.ShapeDtypeStruct((B, width, L), bf16),
            jax.ShapeDtypeStruct((B, width, L), f32),
        ],
        scratch_shapes=[pltpu.VMEM((n_i, tm, D), bf16)],
        compiler_params=_params(("arbitrary", "arbitrary", "arbitrary")),
        name="in_proj",
    )(x, g1, w_in, gq, gk)


LOG2E = 1.4426950408889634
V_AUG_ROWS = HEAD_DIM + 16
MOBA_PIPE_DEPTH = 5


def _moba_kernel(sp_ref, q_ref, k_ref, v_ref, o_ref, ktok_scr, vaug_scr, s_scr, *, single_pass):
    L = q_ref.shape[2]
    blk = MOBA_BLOCK
    nb = L // blk
    top_k = min(MOBA_TOPK, nb - 1)
    qscale = HEAD_DIM ** -0.5 * LOG2E
    pair = pl.program_id(1)

    r_i = lax.broadcasted_iota(jnp.int32, (blk, blk), 0)
    c_i = lax.broadcasted_iota(jnp.int32, (blk, blk), 1)
    causal = r_i <= c_i
    blk_id = lax.broadcasted_iota(jnp.int32, (nb, blk), 0)
    q_row = lax.broadcasted_iota(jnp.int32, (HEAD_DIM, blk), 0)
    q_off = lax.broadcasted_iota(jnp.int32, (1, blk), 1).astype(f32)

    k_t = k_ref[0]
    ktok = k_t.T
    kmean = jnp.concatenate(
        [jnp.mean(ktok[j * blk:(j + 1) * blk], axis=0, keepdims=True) for j in range(nb)], axis=0)
    lane = lax.broadcasted_iota(jnp.int32, ktok.shape, 1)
    key_off = jnp.bitwise_and(lax.broadcasted_iota(jnp.int32, ktok.shape, 0), blk - 1).astype(f32)
    ones_rows = jnp.where(lax.broadcasted_iota(jnp.int32, (V_AUG_ROWS - HEAD_DIM, L), 0) == 0, 1.0, 0.0).astype(bf16)
    for hh in range(2):
        kh = ktok if hh == 0 else pltpu.roll(ktok, HEAD_DIM, 1)
        ktok_scr[hh] = jnp.where(lane < HEAD_DIM, kh, jnp.where(lane < HEAD_DIM + 3, key_off, 0.0)).astype(bf16)
        vaug_scr[hh, 0:HEAD_DIM, :] = v_ref[0, hh * HEAD_DIM:(hh + 1) * HEAD_DIM, :]
        vaug_scr[hh, HEAD_DIM:, :] = ones_rows

    @functools.cache
    def head_setup(hh):
        rows = slice(hh * HEAD_DIM, (hh + 1) * HEAD_DIM)
        head = pair * 2 + hh
        pieces = [jnp.full((HEAD_DIM, blk), sp_ref[head, t], f32) for t in range(3)]
        q_extra = jnp.where(q_row == 0, pieces[0], jnp.where(q_row == 1, pieces[1],
                            jnp.where(q_row == 2, pieces[2], 0.0))).astype(bf16)
        slope2 = sum(jnp.full((1, blk), sp_ref[head, t], f32) for t in range(3))
        kmax = []
        if single_pass:
            kh_t = k_t[rows, :]
            kn2 = jnp.sum(kh_t * kh_t, axis=0, keepdims=True)
            for j in range(nb):
                bj = jnp.sqrt(jnp.max(kn2[:, j * blk:(j + 1) * blk], axis=1, keepdims=True))
                kmax.append(bj if j == 0 else jnp.maximum(kmax[-1], bj))
        return rows, q_extra, slope2, kmax

    @functools.cache
    def block_setup(hh, i):
        rows, q_extra, slope2, kmax = head_setup(hh)
        qf = q_ref[0, rows, i * blk:(i + 1) * blk]
        q_aug = jnp.concatenate([(qf * qscale).astype(bf16), q_extra], axis=0)

        def scores(j):
            s = jnp.dot(ktok_scr[hh, j * blk:(j + 1) * blk, :], q_aug, preferred_element_type=f32)
            return jnp.where(causal, s, NEG) if j == i else s

        sel_off = []
        if i > 0:
            z = jnp.zeros_like(qf)
            q_pair = jnp.concatenate([qf, z] if hh == 0 else [z, qf], axis=0)
            gate = jnp.dot(kmean, q_pair, precision=HIGHEST, preferred_element_type=f32)
            rank = jnp.zeros((nb, blk), f32)
            for jp in range(i):
                row = gate[jp:jp + 1, :]
                beats = (row > gate) | ((row == gate) & (jp < blk_id))
                rank = rank + jnp.where(beats, 1.0, 0.0)
            sel = jnp.where((blk_id < i) & (rank < top_k), 1.0, 0.0)
            sel_off = [(sel[j:j + 1, :] > 0.5, slope2 * float((j - i) * blk)) for j in range(i)]
        shifts = None
        if single_pass:
            q_norm = jnp.sqrt(jnp.sum(qf * qf, axis=0, keepdims=True))
            m = (qscale * q_norm) * kmax[i] + slope2 * q_off
            shifts = [jnp.where(selj, cj - m, NEG) for selj, cj in sel_off] + [-m]
        return scores, sel_off, shifts

    def finish(hh, i, acc):
        rows = head_setup(hh)[0]
        o_ref[0, rows, i * blk:(i + 1) * blk] = (acc[0:HEAD_DIM] / acc[HEAD_DIM:HEAD_DIM + 1]).astype(o_ref.dtype)

    def weighted_values(hh, j, s, shift):
        p = jnp.exp2(s + shift).astype(bf16)
        return jnp.dot(vaug_scr[hh, :, j * blk:(j + 1) * blk], p, preferred_element_type=f32)

    if single_pass:
        order = [(hh, i, j) for hh in range(2) for i in range(nb) for j in range(i + 1)]
        depth = MOBA_PIPE_DEPTH

        def issue_scores(n):
            hh, i, j = order[n]
            s_scr[n % (depth + 1)] = block_setup(hh, i)[0](j)

        for n in range(depth):
            issue_scores(n)
        acc = None
        for n, (hh, i, j) in enumerate(order):
            if n + depth < len(order):
                issue_scores(n + depth)
            d = weighted_values(hh, j, s_scr[n % (depth + 1)], block_setup(hh, i)[2][j])
            acc = d if j == 0 else acc + d
            if j == i:
                finish(hh, i, acc)
    else:
        for hh in range(2):
            for i in range(nb):
                scores, sel_off, _ = block_setup(hh, i)
                s = [scores(j) for j in range(i + 1)]
                m = jnp.max(s[i], axis=0, keepdims=True)
                for j in range(i):
                    selj, cj = sel_off[j]
                    m = jnp.maximum(m, jnp.where(selj, jnp.max(s[j], axis=0, keepdims=True) + cj, NEG))
                shifts = [jnp.where(selj, cj - m, NEG) for selj, cj in sel_off] + [-m]
                acc = None
                for j in range(i + 1):
                    d = weighted_values(hh, j, s[j], shifts[j])
                    acc = d if acc is None else acc + d
                finish(hh, i, acc)


SINGLE_PASS_MAX_GAP = 64.0


def _moba(q_t, k_t, v_t, gq, gk):
    B, width, L = q_t.shape
    n_heads = width // HEAD_DIM
    assert n_heads % 2 == 0 and L % MOBA_BLOCK == 0 and MOBA_BLOCK & (MOBA_BLOCK - 1) == 0
    slope2 = jnp.exp2(-8.0 * (jnp.arange(n_heads, dtype=f32) + 1.0) / n_heads) * LOG2E
    hi = slope2.astype(bf16).astype(f32)
    mid = (slope2 - hi).astype(bf16).astype(f32)
    lo = (slope2 - hi - mid).astype(bf16).astype(f32)
    pieces = jnp.stack([hi, mid, lo], axis=1)
    spec = pl.BlockSpec((1, 2 * HEAD_DIM, L), lambda b, h: (b, h, 0))

    def call(single_pass):
        return pl.pallas_call(
            functools.partial(_moba_kernel, single_pass=single_pass),
            grid=(B, n_heads // 2),
            in_specs=[pl.BlockSpec(memory_space=pltpu.SMEM), spec, spec, spec],
            out_specs=spec,
            out_shape=jax.ShapeDtypeStruct((B, width, L), bf16),
            scratch_shapes=[
                pltpu.VMEM((2, L, 2 * HEAD_DIM), bf16),
                pltpu.VMEM((2, V_AUG_ROWS, L), bf16),
                pltpu.VMEM((MOBA_PIPE_DEPTH + 1, MOBA_BLOCK, MOBA_BLOCK), f32),
            ],
            compiler_params=_params(("arbitrary", "arbitrary")),
            name="moba_single_pass" if single_pass else "moba_two_pass",
        )(pieces, q_t, k_t, v_t)

    gap = 2.0 * (HEAD_DIM ** -0.5 * LOG2E) * HEAD_DIM * jnp.max(jnp.abs(gq)) * jnp.max(jnp.abs(gk))
    return lax.cond(gap <= SINGLE_PASS_MAX_GAP, lambda: call(True), lambda: call(False))


SSM_ROW_PITCH = 24


def _gelu_tanh(x):
    c = math.sqrt(2.0 / math.pi)
    w = x * ((-2.0 * c * 0.044715 * LOG2E) * (x * x) + (-2.0 * c * LOG2E))
    return x / (1.0 + jnp.exp2(w))


def _ssm_kernel(u_ref, par_ref, dexp_ref, y_ref, lhs_scr, m_scr, ws_scr, wo_scr, sp_scr, flat_scr):
    B = u_ref.shape[0]
    L = u_ref.shape[2]
    T = SSM_CHUNK
    LT = 2 * T
    P = SSM_STATE
    Hc = SSM_GROUP
    nC = L // T
    nLT = L // LT
    pitch = SSM_ROW_PITCH
    lane = lax.broadcasted_iota(jnp.int32, (B, LT), 1)

    def pair_lo(a, b):
        return jnp.where(lane[:a.shape[0]] < T, a, pltpu.roll(b, T, 1))

    def pair_hi(a, b):
        return jnp.where(lane[:a.shape[0]] < T, pltpu.roll(a, T, 1), b)

    for c in range(nLT):
        for b in range(B):
            flat_scr[(c * B + b) * pitch:(c * B + b) * pitch + Hc, :] = u_ref[b, :, c * LT:(c + 1) * LT]
    for c in range(nLT):
        for hp in range(0, Hc, 2):
            ue = flat_scr[pl.ds(c * B * pitch + hp, B, stride=pitch), :]
            uo = flat_scr[pl.ds(c * B * pitch + hp + 1, B, stride=pitch), :]
            lhs_scr[(2 * c) * B:(2 * c + 1) * B, hp * T:(hp + 2) * T] = pair_lo(ue, uo).astype(bf16)
            lhs_scr[(2 * c + 1) * B:(2 * c + 2) * B, hp * T:(hp + 2) * T] = pair_hi(ue, uo).astype(bf16)

    P2 = 2 * P
    lr_r, li_r = par_ref[0, 0:1, 0:P], par_ref[0, 1:2, 0:P]
    dt = jnp.exp(par_ref[0, 2:3, 0:1])
    bt_re, bt_im = par_ref[0, 8:8 + Hc, 0:P], par_ref[0, 8:8 + Hc, P:P2]
    c_re, c_im = par_ref[0, 8 + Hc:8 + 2 * Hc, 0:P], par_ref[0, 8 + Hc:8 + 2 * Hc, P:P2]
    col0 = 8 + 2 * Hc
    lr_c, li_c = par_ref[0, col0:col0 + P, 0:1], par_ref[0, col0:col0 + P, 1:2]
    ct_re, ct_im = par_ref[0, col0:col0 + P, Hc:2 * Hc], par_ref[0, col0:col0 + P, 2 * Hc:3 * Hc]

    def lam_pow(lr, li, n):
        mag = jnp.exp(n * (lr * dt))
        ang = n * (li * dt)
        return mag * jnp.cos(ang), mag * jnp.sin(ang)

    ab_re, ab_im = lam_pow(lr_r, li_r, 1.0)
    den = lr_r * lr_r + li_r * li_r
    nr = ab_re - 1.0
    f_re = (nr * lr_r + ab_im * li_r) / den
    f_im = (ab_im * lr_r - nr * li_r) / den
    bb_re = f_re * bt_re - f_im * bt_im
    bb_im = f_re * bt_im + f_im * bt_re

    d_lane = lax.broadcasted_iota(jnp.int32, (P, LT), 1).astype(f32)
    pd_re, pd_im = lam_pow(lr_c, li_c, d_lane)

    cb_re = (c_re[:, None, :] * bb_re[None, :, :] - c_im[:, None, :] * bb_im[None, :, :]).reshape(Hc * Hc, P)
    cb_im = (c_re[:, None, :] * bb_im[None, :, :] + c_im[:, None, :] * bb_re[None, :, :]).reshape(Hc * Hc, P)
    kt = (jnp.dot(cb_re, pd_re, precision=HIGHEST, preferred_element_type=f32)
          - jnp.dot(cb_im, pd_im, precision=HIGHEST, preferred_element_type=f32))
    kt4 = kt.reshape(Hc // 2, 2, Hc, LT)
    k_even, k_odd = kt4[:, 0].reshape(Hc // 2 * Hc, LT), kt4[:, 1].reshape(Hc // 2 * Hc, LT)
    lane_k = lax.broadcasted_iota(jnp.int32, k_even.shape, 1)
    kp = jnp.where(lane_k < T, k_even, pltpu.roll(k_odd, T, 1))

    flip = jnp.where(lax.broadcasted_iota(jnp.int32, (T, LT), 0) + lax.broadcasted_iota(jnp.int32, (T, LT), 1) == T - 1,
                     1.0, 0.0)
    rev_t = lambda a: lax.dot_general(flip, a, (((1,), (1,)), ((), ())), precision=HIGHEST, preferred_element_type=f32)
    pe_re, pe_im = rev_t(pd_re), rev_t(pd_im)
    for hp in range(Hc):
        br, bi = bb_re[hp:hp + 1, :], bb_im[hp:hp + 1, :]
        ws_scr[hp * T:(hp + 1) * T, :] = jnp.concatenate(
            [pe_re * br - pe_im * bi, pe_re * bi + pe_im * br], axis=1).astype(bf16)
    l1_re, l1_im = pd_re[:, 1:2], pd_im[:, 1:2]
    po_re, po_im = pd_re * l1_re - pd_im * l1_im, pd_re * l1_im + pd_im * l1_re
    lane_p = lax.broadcasted_iota(jnp.int32, (2 * P, LT), 1)

    def wo_block(h):
        cre, cim = ct_re[:, h:h + 1], ct_im[:, h:h + 1]
        return jnp.concatenate([cre * po_re - cim * po_im, -(cre * po_im + cim * po_re)], axis=0)

    for h in range(0, Hc, 2):
        wo_scr[:, h * T:(h + 2) * T] = jnp.where(lane_p < T, wo_block(h), pltpu.roll(wo_block(h + 1), T, 1)).astype(bf16)

    v = jnp.dot(lhs_scr[...], ws_scr[...], preferred_element_type=f32)
    a_re, a_im = lam_pow(lr_r, li_r, float(T))
    a1 = jnp.concatenate([a_re, a_re], axis=1)
    a2 = jnp.concatenate([-a_im, a_im], axis=1)
    v_sw = pltpu.roll(v, P, 1)
    s = jnp.zeros((B, 2 * P), f32)
    s_sw = jnp.zeros((B, 2 * P), f32)
    for c in range(nC):
        sp_scr[c * B:(c + 1) * B, :] = s.astype(bf16)
        s, s_sw = (a1 * s + a2 * s_sw + v[c * B:(c + 1) * B, :],
                   a1 * s_sw - a2 * s + v_sw[c * B:(c + 1) * B, :])

    tau_i = lax.broadcasted_iota(jnp.int32, (T, LT), 0)
    t_i = jnp.bitwise_and(lax.broadcasted_iota(jnp.int32, (T, LT), 1), T - 1)
    causal = t_i >= tau_i

    for h0 in range(0, Hc, 4):
        cols = slice(h0 * T, (h0 + 4) * T)
        for h in (h0, h0 + 2):
            for hp in range(Hc):
                row = kp[(h // 2) * Hc + hp:(h // 2) * Hc + hp + 1, :]
                tile = pltpu.roll(jnp.broadcast_to(row, (T, LT)), 0, 1, stride=1, stride_axis=0)
                m_scr[hp * T:(hp + 1) * T, h * T:(h + 2) * T] = jnp.where(causal, tile, 0.0).astype(bf16)
        lhs4 = lhs_scr[:, cols]
        y = jnp.dot(lhs_scr[...], m_scr[:, cols], preferred_element_type=f32)
        y = y + jnp.dot(sp_scr[...], wo_scr[:, cols], preferred_element_type=f32)
        y = _gelu_tanh(y + dexp_ref[0, :, cols] * lhs4.astype(f32))
        for c in range(nLT):
            for k, h in enumerate((h0, h0 + 2)):
                y0 = y[(2 * c) * B:(2 * c + 1) * B, k * LT:(k + 1) * LT]
                y1 = y[(2 * c + 1) * B:(2 * c + 2) * B, k * LT:(k + 1) * LT]
                flat_scr[pl.ds(c * B * pitch + h, B, stride=pitch), :] = pair_lo(y0, y1)
                flat_scr[pl.ds(c * B * pitch + h + 1, B, stride=pitch), :] = pair_hi(y0, y1)
    for c in range(nLT):
        for b in range(B):
            y_ref[b, :, c * LT:(c + 1) * LT] = flat_scr[(c * B + b) * pitch:(c * B + b) * pitch + Hc, :].astype(y_ref.dtype)


def _ssm_pack(lambda_re, lambda_im, log_dt, b_re, b_im, c_re, c_im):
    G, P = lambda_re.shape
    Hc = SSM_GROUP
    assert 2 * P == 128 and 3 * Hc <= 128
    lanes = lambda a: jnp.pad(a, ((0, 0), (0, 0), (0, 128 - a.shape[2])))
    head = jnp.pad(jnp.stack([lanes(lambda_re[:, None, :])[:, 0], lanes(lambda_im[:, None, :])[:, 0],
                              lanes(log_dt[:, None, None])[:, 0]], axis=1), ((0, 0), (0, 5), (0, 0)))
    b_t = jnp.concatenate([jnp.swapaxes(b_re, 1, 2), jnp.swapaxes(b_im, 1, 2)], axis=2)
    c_n = jnp.concatenate([c_re, c_im], axis=2)
    cols = jnp.concatenate([lambda_re[:, :, None], lambda_im[:, :, None], jnp.zeros((G, P, Hc - 2), f32),
                            jnp.swapaxes(c_re, 1, 2), jnp.swapaxes(c_im, 1, 2)], axis=2)
    return jnp.concatenate([head, b_t, c_n, lanes(cols)], axis=1)


def _ssm(u_t, lambda_re, lambda_im, log_dt, b_re, b_im, c_re, c_im, d_skip):
    B, width, L = u_t.shape
    G, P = lambda_re.shape
    Hc, T = SSM_GROUP, SSM_CHUNK
    assert width == G * Hc and P == SSM_STATE and L % T == 0
    nC = L // T
    params = _ssm_pack(lambda_re, lambda_im, log_dt, b_re, b_im, c_re, c_im)
    d_exp = jnp.repeat(d_skip.reshape(G, 1, Hc), T, axis=2)
    g3 = lambda shp: pl.BlockSpec((1,) + shp, lambda g: (g, 0, 0))
    u_spec = pl.BlockSpec((B, Hc, L), lambda g: (0, g, 0))
    return pl.pallas_call(
        _ssm_kernel,
        grid=(G,),
        in_specs=[u_spec, g3(params.shape[1:]), g3((1, Hc * T))],
        out_specs=u_spec,
        out_shape=jax.ShapeDtypeStruct((B, width, L), bf16),
        scratch_shapes=[
            pltpu.VMEM((nC * B, Hc * T), bf16),
            pltpu.VMEM((Hc * T, Hc * T), bf16),
            pltpu.VMEM((Hc * T, 2 * P), bf16),
            pltpu.VMEM((2 * P, Hc * T), bf16),
            pltpu.VMEM((nC * B, 2 * P), bf16),
            pltpu.VMEM((L // (2 * T) * B * SSM_ROW_PITCH, 2 * T), f32),
        ],
        compiler_params=_params(("arbitrary",)),
        name="ssm",
    )(u_t, params, d_exp)


MIX_CHUNK = 256


def _mix_kernel(x_ref, a_ref, y_ref, wg_ref, bg_ref, ga_ref, gs_ref, wo_ref, o_ref):
    def rms_rows(t, g_ref):
        return t * lax.rsqrt(jnp.mean(t * t, axis=0, keepdims=True) + EPS) * g_ref[...]

    for c in range(0, x_ref.shape[1], MIX_CHUNK):
        tok = slice(c, c + MIX_CHUNK)
        a_n = rms_rows(a_ref[0, :, tok].astype(f32), ga_ref)
        yg = y_ref[0, :, tok]
        z = jnp.dot(wg_ref[...], yg, preferred_element_type=f32) + bg_ref[...]
        s_n = rms_rows(yg.astype(f32) * jax.nn.sigmoid(z), gs_ref)
        mixed_t = jnp.concatenate([a_n, s_n], axis=0).astype(bf16)
        delta = lax.dot_general(mixed_t, wo_ref[...], (((0,), (0,)), ((), ())), preferred_element_type=f32)
        o_ref[0, tok, :] = x_ref[0, tok, :] + delta


def _mix(x, attn_t, yg_t, w_glu_t, b_glu, g_attn, g_ssm, w_out, tn=1024):
    B, L, D = x.shape
    aw, sw = attn_t.shape[1], yg_t.shape[1]
    assert aw + sw == D and L % tn == 0 and tn % MIX_CHUNK == 0
    const = lambda shp: pl.BlockSpec(shp, lambda b, i: (0, 0), pipeline_mode=pl.Buffered(1))
    return pl.pallas_call(
        _mix_kernel,
        grid=(B, L // tn),
        in_specs=[
            pl.BlockSpec((1, tn, D), lambda b, i: (b, i, 0)),
            pl.BlockSpec((1, aw, tn), lambda b, i: (b, 0, i)),
            pl.BlockSpec((1, sw, tn), lambda b, i: (b, 0, i)),
            const((sw, sw)), const((sw, 1)), const((aw, 1)), const((sw, 1)), const((D, D)),
        ],
        out_specs=pl.BlockSpec((1, tn, D), lambda b, i: (b, i, 0)),
        out_shape=jax.ShapeDtypeStruct((B, L, D), f32),
        compiler_params=_params(("arbitrary", "arbitrary")),
        name="mix",
    )(x, attn_t, yg_t, w_glu_t, b_glu, g_attn, g_ssm, w_out)


def _ffn_kernel(x_ref, g2_ref, w1_ref, w2_ref, o_ref, h_scr):
    f = pl.program_id(1)

    @pl.when(f == 0)
    def _():
        x = x_ref[...]
        ms = jnp.mean(x * x, axis=-1, keepdims=True)
        h_scr[...] = (x * lax.rsqrt(ms + EPS) * g2_ref[...]).astype(bf16)
        o_ref[...] = x

    a = jnp.dot(h_scr[...], w1_ref[...], preferred_element_type=f32)
    a = jnp.square(jnp.maximum(a, 0.0)).astype(bf16)
    o_ref[...] += jnp.dot(a, w2_ref[...], preferred_element_type=f32)


def _ffn(x2d, g2, w1, w2, tm=512, tf=1024):
    N, D = x2d.shape
    F = w1.shape[1]
    assert N % tm == 0 and F % tf == 0
    return pl.pallas_call(
        _ffn_kernel,
        grid=(N // tm, F // tf),
        in_specs=[
            pl.BlockSpec((tm, D), lambda i, f: (i, 0)),
            pl.BlockSpec((1, D), lambda i, f: (0, 0)),
            pl.BlockSpec((D, tf), lambda i, f: (0, f)),
            pl.BlockSpec((tf, D), lambda i, f: (f, 0)),
        ],
        out_specs=pl.BlockSpec((tm, D), lambda i, f: (i, 0)),
        out_shape=jax.ShapeDtypeStruct((N, D), f32),
        scratch_shapes=[pltpu.VMEM((tm, D), bf16)],
        compiler_params=_params(("arbitrary", "arbitrary")),
        name="ffn",
    )(x2d, g2, w1, w2)


def _layer(x, norm1_gain, w_in, q_norm_gain, k_norm_gain, attn_out_gain, lambda_re, lambda_im, log_dt,
           b_re, b_im, c_re, c_im, d_skip, w_glu, b_glu, ssm_out_gain, w_out, norm2_gain, w_ff1, w_ff2):
    B, L, D = x.shape
    sw = d_skip.shape[0]
    aw = D - sw
    assert aw == sw and w_in.shape == (D, 3 * aw + sw)
    q_t, k_t, v_t, u_t = _in_proj(
        x, norm1_gain.reshape(1, D), w_in.astype(bf16),
        q_norm_gain.reshape(HEAD_DIM, 1), k_norm_gain.reshape(HEAD_DIM, 1), aw)
    attn_t = _moba(q_t, k_t, v_t, q_norm_gain, k_norm_gain)
    yg_t = _ssm(u_t, lambda_re, lambda_im, log_dt, b_re, b_im, c_re, c_im, d_skip)
    x1 = _mix(x, attn_t, yg_t, w_glu.T.astype(bf16), b_glu.reshape(sw, 1),
              attn_out_gain.reshape(aw, 1), ssm_out_gain.reshape(sw, 1), w_out.astype(bf16))
    out = _ffn(x1.reshape(B * L, D), norm2_gain.reshape(1, D), w_ff1.astype(bf16), w_ff2.astype(bf16))
    return out.reshape(B, L, D)


def kernel(x, norm1_gain, w_in, q_norm_gain, k_norm_gain, attn_out_gain, lambda_re, lambda_im, log_dt, b_re, b_im, c_re, c_im, d_skip, w_glu, b_glu, ssm_out_gain, w_out, norm2_gain, w_ff1, w_ff2):
    per_layer = (norm1_gain, w_in, q_norm_gain, k_norm_gain, attn_out_gain, lambda_re, lambda_im, log_dt,
                 b_re, b_im, c_re, c_im, d_skip, w_glu, b_glu, ssm_out_gain, w_out, norm2_gain, w_ff1, w_ff2)
    for i in range(w_in.shape[0]):
        x = _layer(x, *(p[i] for p in per_layer))
    return x
```

```python
import functools
import math

import jax
import jax.numpy as jnp
from jax import lax
from jax.experimental import pallas as pl
from jax.experimental.pallas import tpu as pltpu

f32 = jnp.float32
bf16 = jnp.bfloat16

EPS = 1e-6
NEG = -1e30
HEAD_DIM = 64
MOBA_BLOCK = 256
MOBA_TOPK = 3
SSM_GROUP = 16
SSM_STATE = 64
SSM_CHUNK = 64
VMEM_LIMIT = 56 * 1024 * 1024
HIGHEST = lax.Precision.HIGHEST


def _params(sem):
    return pltpu.CompilerParams(dimension_semantics=sem, vmem_limit_bytes=VMEM_LIMIT)


def _in_proj_kernel(x_ref, g1_ref, w_ref, gq_ref, gk_ref, q_ref, k_ref, v_ref, u_ref, h_scr):
    j = pl.program_id(1)
    i = pl.program_id(2)
    tm = x_ref.shape[1]

    @pl.when(j == 0)
    def _():
        x = x_ref[0]
        ms = jnp.mean(x * x, axis=-1, keepdims=True)
        h_scr[i] = (x * lax.rsqrt(ms + EPS) * g1_ref[...]).astype(bf16)

    def project(out_ref, finish):
        t = lax.dot_general(w_ref[...], h_scr[i], (((0,), (1,)), ((), ())), preferred_element_type=f32)
        out_ref[0] = finish(t)

    def head_norm(g_ref):
        def finish(t):
            t3 = t.reshape(t.shape[0] // HEAD_DIM, HEAD_DIM, tm)
            ms = jnp.mean(t3 * t3, axis=1, keepdims=True)
            return (t3 * lax.rsqrt(ms + EPS) * g_ref[...][None]).reshape(t.shape)
        return finish

    @pl.when(j == 0)
    def _():
        project(q_ref, head_norm(gq_ref))

    @pl.when(j == 1)
    def _():
        project(k_ref, head_norm(gk_ref))

    @pl.when(j == 2)
    def _():
        project(v_ref, lambda t: t.astype(bf16))

    @pl.when(j == 3)
    def _():
        project(u_ref, lambda t: t)


def _in_proj(x, g1, w_in, gq, gk, width, tm=512):
    B, L, D = x.shape
    assert w_in.shape == (D, 4 * width) and L % tm == 0
    n_i = L // tm
    last = n_i - 1

    def out_spec(jj):
        return pl.BlockSpec((1, width, tm),
                            lambda b, j, i: (b, 0, jnp.where(j == jj, i, jnp.where(j < jj, 0, last))))

    return pl.pallas_call(
        _in_proj_kernel,
        grid=(B, 4, n_i),
        in_specs=[
            pl.BlockSpec((1, tm, D), lambda b, j, i: (b, jnp.where(j == 0, i, last), 0)),
            pl.BlockSpec((1, D), lambda b, j, i: (0, 0)),
            pl.BlockSpec((D, width), lambda b, j, i: (0, j)),
            pl.BlockSpec((HEAD_DIM, 1), lambda b, j, i: (0, 0)),
            pl.BlockSpec((HEAD_DIM, 1), lambda b, j, i: (0, 0)),
        ],
        out_specs=[out_spec(0), out_spec(1), out_spec(2), out_spec(3)],
        out_shape=[
            jax.ShapeDtypeStruct((B, width, L), f32),
            jax.ShapeDtypeStruct((B, width, L), f32),
            jax.ShapeDtypeStruct((B, width, L), bf16),
            jax.ShapeDtypeStruct((B, width, L), f32),
        ],
        scratch_shapes=[pltpu.VMEM((n_i, tm, D), bf16)],
        compiler_params=_params(("arbitrary", "arbitrary", "arbitrary")),
        name="in_proj",
    )(x, g1, w_in, gq, gk)


LOG2E = 1.4426950408889634
V_AUG_ROWS = HEAD_DIM + 16
MOBA_PIPE_DEPTH = 5


def _moba_kernel(sp_ref, q_ref, k_ref, v_ref, o_ref, ktok_scr, vaug_scr, s_scr, *, single_pass):
    L = q_ref.shape[2]
    blk = MOBA_BLOCK
    nb = L // blk
    top_k = min(MOBA_TOPK, nb - 1)
    qscale = HEAD_DIM ** -0.5 * LOG2E
    pair = pl.program_id(1)

    r_i = lax.broadcasted_iota(jnp.int32, (blk, blk), 0)
    c_i = lax.broadcasted_iota(jnp.int32, (blk, blk), 1)
    causal = r_i <= c_i
    blk_id = lax.broadcasted_iota(jnp.int32, (nb, blk), 0)
    q_row = lax.broadcasted_iota(jnp.int32, (HEAD_DIM, blk), 0)
    q_off = lax.broadcasted_iota(jnp.int32, (1, blk), 1).astype(f32)

    k_t = k_ref[0]
    ktok = k_t.T
    kmean = jnp.concatenate(
        [jnp.mean(ktok[j * blk:(j + 1) * blk], axis=0, keepdims=True) for j in range(nb)], axis=0)
    lane = lax.broadcasted_iota(jnp.int32, ktok.shape, 1)
    key_off = jnp.bitwise_and(lax.broadcasted_iota(jnp.int32, ktok.shape, 0), blk - 1).astype(f32)
    ones_rows = jnp.where(lax.broadcasted_iota(jnp.int32, (V_AUG_ROWS - HEAD_DIM, L), 0) == 0, 1.0, 0.0).astype(bf16)
    for hh in range(2):
        kh = ktok if hh == 0 else pltpu.roll(ktok, HEAD_DIM, 1)
        ktok_scr[hh] = jnp.where(lane < HEAD_DIM, kh, jnp.where(lane < HEAD_DIM + 3, key_off, 0.0)).astype(bf16)
        vaug_scr[hh, 0:HEAD_DIM, :] = v_ref[0, hh * HEAD_DIM:(hh + 1) * HEAD_DIM, :]
        vaug_scr[hh, HEAD_DIM:, :] = ones_rows

    @functools.cache
    def head_setup(hh):
        rows = slice(hh * HEAD_DIM, (hh + 1) * HEAD_DIM)
        head = pair * 2 + hh
        pieces = [jnp.full((HEAD_DIM, blk), sp_ref[head, t], f32) for t in range(3)]
        q_extra = jnp.where(q_row == 0, pieces[0], jnp.where(q_row == 1, pieces[1],
                            jnp.where(q_row == 2, pieces[2], 0.0))).astype(bf16)
        slope2 = sum(jnp.full((1, blk), sp_ref[head, t], f32) for t in range(3))
        kmax = []
        if single_pass:
            kh_t = k_t[rows, :]
            kn2 = jnp.sum(kh_t * kh_t, axis=0, keepdims=True)
            for j in range(nb):
                bj = jnp.sqrt(jnp.max(kn2[:, j * blk:(j + 1) * blk], axis=1, keepdims=True))
                kmax.append(bj if j == 0 else jnp.maximum(kmax[-1], bj))
        return rows, q_extra, slope2, kmax

    @functools.cache
    def block_setup(hh, i):
        rows, q_extra, slope2, kmax = head_setup(hh)
        qf = q_ref[0, rows, i * blk:(i + 1) * blk]
        q_aug = jnp.concatenate([(qf * qscale).astype(bf16), q_extra], axis=0)

        def scores(j):
            s = jnp.dot(ktok_scr[hh, j * blk:(j + 1) * blk, :], q_aug, preferred_element_type=f32)
            return jnp.where(causal, s, NEG) if j == i else s

        sel_off = []
        if i > 0:
            z = jnp.zeros_like(qf)
            q_pair = jnp.concatenate([qf, z] if hh == 0 else [z, qf], axis=0)
            gate = jnp.dot(kmean, q_pair, precision=HIGHEST, preferred_element_type=f32)
            rank = jnp.zeros((nb, blk), f32)
            for jp in range(i):
                row = gate[jp:jp + 1, :]
                beats = (row > gate) | ((row == gate) & (jp < blk_id))
                rank = rank + jnp.where(beats, 1.0, 0.0)
            sel = jnp.where((blk_id < i) & (rank < top_k), 1.0, 0.0)
            sel_off = [(sel[j:j + 1, :] > 0.5, slope2 * float((j - i) * blk)) for j in range(i)]
        shifts = None
        if single_pass:
            q_norm = jnp.sqrt(jnp.sum(qf * qf, axis=0, keepdims=True))
            m = (qscale * q_norm) * kmax[i] + slope2 * q_off
            shifts = [jnp.where(selj, cj - m, NEG) for selj, cj in sel_off] + [-m]
        return scores, sel_off, shifts

    def finish(hh, i, acc):
        rows = head_setup(hh)[0]
        o_ref[0, rows, i * blk:(i + 1) * blk] = (acc[0:HEAD_DIM] / acc[HEAD_DIM:HEAD_DIM + 1]).astype(o_ref.dtype)

    def weighted_values(hh, j, s, shift):
        p = jnp.exp2(s + shift).astype(bf16)
        return jnp.dot(vaug_scr[hh, :, j * blk:(j + 1) * blk], p, preferred_element_type=f32)

    if single_pass:
        order = [(hh, i, j) for hh in range(2) for i in range(nb) for j in range(i + 1)]
        depth = MOBA_PIPE_DEPTH

        def issue_scores(n):
            hh, i, j = order[n]
            s_scr[n % (depth + 1)] = block_setup(hh, i)[0](j)

        for n in range(depth):
            issue_scores(n)
        acc = None
        for n, (hh, i, j) in enumerate(order):
            if n + depth < len(order):
                issue_scores(n + depth)
            d = weighted_values(hh, j, s_scr[n % (depth + 1)], block_setup(hh, i)[2][j])
            acc = d if j == 0 else acc + d
            if j == i:
                finish(hh, i, acc)
    else:
        for hh in range(2):
            for i in range(nb):
                scores, sel_off, _ = block_setup(hh, i)
                s = [scores(j) for j in range(i + 1)]
                m = jnp.max(s[i], axis=0, keepdims=True)
                for j in range(i):
                    selj, cj = sel_off[j]
                    m = jnp.maximum(m, jnp.where(selj, jnp.max(s[j], axis=0, keepdims=True) + cj, NEG))
                shifts = [jnp.where(selj, cj - m, NEG) for selj, cj in sel_off] + [-m]
                acc = None
                for j in range(i + 1):
                    d = weighted_values(hh, j, s[j], shifts[j])
                    acc = d if acc is None else acc + d
                finish(hh, i, acc)


SINGLE_PASS_MAX_GAP = 64.0


def _moba(q_t, k_t, v_t, gq, gk):
    B, width, L = q_t.shape
    n_heads = width // HEAD_DIM
    assert n_heads % 2 == 0 and L % MOBA_BLOCK == 0 and MOBA_BLOCK & (MOBA_BLOCK - 1) == 0
    slope2 = jnp.exp2(-8.0 * (jnp.arange(n_heads, dtype=f32) + 1.0) / n_heads) * LOG2E
    hi = slope2.astype(bf16).astype(f32)
    mid = (slope2 - hi).astype(bf16).astype(f32)
    lo = (slope2 - hi - mid).astype(bf16).astype(f32)
    pieces = jnp.stack([hi, mid, lo], axis=1)
    spec = pl.BlockSpec((1, 2 * HEAD_DIM, L), lambda b, h: (b, h, 0))

    def call(single_pass):
        return pl.pallas_call(
            functools.partial(_moba_kernel, single_pass=single_pass),
            grid=(B, n_heads // 2),
            in_specs=[pl.BlockSpec(memory_space=pltpu.SMEM), spec, spec, spec],
            out_specs=spec,
            out_shape=jax.ShapeDtypeStruct((B, width, L), bf16),
            scratch_shapes=[
                pltpu.VMEM((2, L, 2 * HEAD_DIM), bf16),
                pltpu.VMEM((2, V_AUG_ROWS, L), bf16),
                pltpu.VMEM((MOBA_PIPE_DEPTH + 1, MOBA_BLOCK, MOBA_BLOCK), f32),
            ],
            compiler_params=_params(("arbitrary", "arbitrary")),
            name="moba_single_pass" if single_pass else "moba_two_pass",
        )(pieces, q_t, k_t, v_t)

    gap = 2.0 * (HEAD_DIM ** -0.5 * LOG2E) * HEAD_DIM * jnp.max(jnp.abs(gq)) * jnp.max(jnp.abs(gk))
    return lax.cond(gap <= SINGLE_PASS_MAX_GAP, lambda: call(True), lambda: call(False))


SSM_ROW_PITCH = 24


def _gelu_tanh(x):
    c = math.sqrt(2.0 / math.pi)
    w = x * ((-2.0 * c * 0.044715 * LOG2E) * (x * x) + (-2.0 * c * LOG2E))
    return x / (1.0 + jnp.exp2(w))


def _ssm_kernel(u_ref, par_ref, dexp_ref, y_ref, lhs_scr, m_scr, ws_scr, wo_scr, sp_scr, flat_scr):
    B = u_ref.shape[0]
    L = u_ref.shape[2]
    T = SSM_CHUNK
    LT = 2 * T
    P = SSM_STATE
    Hc = SSM_GROUP
    nC = L // T
    nLT = L // LT
    pitch = SSM_ROW_PITCH
    lane = lax.broadcasted_iota(jnp.int32, (B, LT), 1)

    def regroup(a, b):
        low = lane[:a.shape[0]] < T
        swapped = pltpu.roll(jnp.where(low, b, a), T, 1)
        return jnp.where(low, a, swapped), jnp.where(low, swapped, b)

    for c in range(nLT):
        for b in range(B):
            flat_scr[(c * B + b) * pitch:(c * B + b) * pitch + Hc, :] = u_ref[b, :, c * LT:(c + 1) * LT]
    for c in range(nLT):
        for hp in range(0, Hc, 2):
            ue = flat_scr[pl.ds(c * B * pitch + hp, B, stride=pitch), :]
            uo = flat_scr[pl.ds(c * B * pitch + hp + 1, B, stride=pitch), :]
            first, second = regroup(ue, uo)
            lhs_scr[(2 * c) * B:(2 * c + 1) * B, hp * T:(hp + 2) * T] = first.astype(bf16)
            lhs_scr[(2 * c + 1) * B:(2 * c + 2) * B, hp * T:(hp + 2) * T] = second.astype(bf16)

    P2 = 2 * P
    lr_r, li_r = par_ref[0, 0:1, 0:P], par_ref[0, 1:2, 0:P]
    dt = jnp.exp(par_ref[0, 2:3, 0:1])
    bt_re, bt_im = par_ref[0, 8:8 + Hc, 0:P], par_ref[0, 8:8 + Hc, P:P2]
    c_re, c_im = par_ref[0, 8 + Hc:8 + 2 * Hc, 0:P], par_ref[0, 8 + Hc:8 + 2 * Hc, P:P2]
    col0 = 8 + 2 * Hc
    lr_c, li_c = par_ref[0, col0:col0 + P, 0:1], par_ref[0, col0:col0 + P, 1:2]
    ct_re, ct_im = par_ref[0, col0:col0 + P, Hc:2 * Hc], par_ref[0, col0:col0 + P, 2 * Hc:3 * Hc]

    def lam_pow(lr, li, n):
        mag = jnp.exp(n * (lr * dt))
        ang = n * (li * dt)
        return mag * jnp.cos(ang), mag * jnp.sin(ang)

    ab_re, ab_im = lam_pow(lr_r, li_r, 1.0)
    den = lr_r * lr_r + li_r * li_r
    nr = ab_re - 1.0
    f_re = (nr * lr_r + ab_im * li_r) / den
    f_im = (ab_im * lr_r - nr * li_r) / den
    bb_re = f_re * bt_re - f_im * bt_im
    bb_im = f_re * bt_im + f_im * bt_re

    d_lane = jnp.bitwise_and(lax.broadcasted_iota(jnp.int32, (P, LT), 1), T - 1).astype(f32)
    pd_re, pd_im = lam_pow(lr_c, li_c, d_lane)

    cb_re = (c_re[:, None, :] * bb_re[None, :, :] - c_im[:, None, :] * bb_im[None, :, :]).reshape(Hc * Hc, P)
    cb_im = (c_re[:, None, :] * bb_im[None, :, :] + c_im[:, None, :] * bb_re[None, :, :]).reshape(Hc * Hc, P)
    kt = (jnp.dot(cb_re, pd_re, precision=HIGHEST, preferred_element_type=f32)
          - jnp.dot(cb_im, pd_im, precision=HIGHEST, preferred_element_type=f32))
    kt4 = kt.reshape(Hc // 2, 2, Hc, LT)
    k_even, k_odd = kt4[:, 0].reshape(Hc // 2 * Hc, LT), kt4[:, 1].reshape(Hc // 2 * Hc, LT)
    lane_k = lax.broadcasted_iota(jnp.int32, k_even.shape, 1)
    kp = jnp.where(lane_k < T, k_even, k_odd)

    flip = jnp.where(lax.broadcasted_iota(jnp.int32, (T, LT), 0) + lax.broadcasted_iota(jnp.int32, (T, LT), 1) == T - 1,
                     1.0, 0.0)
    rev_t = lambda a: lax.dot_general(flip, a, (((1,), (1,)), ((), ())), precision=HIGHEST, preferred_element_type=f32)
    pe_re, pe_im = rev_t(pd_re), rev_t(pd_im)
    for hp in range(Hc):
        br, bi = bb_re[hp:hp + 1, :], bb_im[hp:hp + 1, :]
        ws_scr[hp * T:(hp + 1) * T, :] = jnp.concatenate(
            [pe_re * br - pe_im * bi, pe_re * bi + pe_im * br], axis=1).astype(bf16)
    l1_re, l1_im = pd_re[:, 1:2], pd_im[:, 1:2]
    po_re, po_im = pd_re * l1_re - pd_im * l1_im, pd_re * l1_im + pd_im * l1_re
    lane_p = lax.broadcasted_iota(jnp.int32, (2 * P, LT), 1)

    def wo_block(h):
        cre, cim = ct_re[:, h:h + 1], ct_im[:, h:h + 1]
        return jnp.concatenate([cre * po_re - cim * po_im, -(cre * po_im + cim * po_re)], axis=0)

    for h in range(0, Hc, 2):
        wo_scr[:, h * T:(h + 2) * T] = jnp.where(lane_p < T, wo_block(h), wo_block(h + 1)).astype(bf16)

    v = jnp.dot(lhs_scr[...], ws_scr[...], preferred_element_type=f32)
    a_re, a_im = lam_pow(lr_r, li_r, float(T))
    a1 = jnp.concatenate([a_re, a_re], axis=1)
    a2 = jnp.concatenate([-a_im, a_im], axis=1)
    v_sw = pltpu.roll(v, P, 1)
    s = jnp.zeros((B, 2 * P), f32)
    s_sw = jnp.zeros((B, 2 * P), f32)
    for c in range(nC):
        sp_scr[c * B:(c + 1) * B, :] = s.astype(bf16)
        s, s_sw = (a1 * s + a2 * s_sw + v[c * B:(c + 1) * B, :],
                   a1 * s_sw - a2 * s + v_sw[c * B:(c + 1) * B, :])

    tau_i = lax.broadcasted_iota(jnp.int32, (T, LT), 0)
    t_i = jnp.bitwise_and(lax.broadcasted_iota(jnp.int32, (T, LT), 1), T - 1)
    causal = t_i >= tau_i

    for h0 in range(0, Hc, 4):
        cols = slice(h0 * T, (h0 + 4) * T)
        for h in (h0, h0 + 2):
            for hp in range(Hc):
                row = kp[(h // 2) * Hc + hp:(h // 2) * Hc + hp + 1, :]
                tile = pltpu.roll(jnp.broadcast_to(row, (T, LT)), 0, 1, stride=1, stride_axis=0)
                m_scr[hp * T:(hp + 1) * T, h * T:(h + 2) * T] = jnp.where(causal, tile, 0.0).astype(bf16)
        lhs4 = lhs_scr[:, cols]
        y = jnp.dot(lhs_scr[...], m_scr[:, cols], preferred_element_type=f32)
        y = y + jnp.dot(sp_scr[...], wo_scr[:, cols], preferred_element_type=f32)
        y = _gelu_tanh(y + dexp_ref[0, :, cols] * lhs4.astype(f32))
        for c in range(nLT):
            for k, h in enumerate((h0, h0 + 2)):
                y0 = y[(2 * c) * B:(2 * c + 1) * B, k * LT:(k + 1) * LT]
                y1 = y[(2 * c + 1) * B:(2 * c + 2) * B, k * LT:(k + 1) * LT]
                even, odd = regroup(y0, y1)
                flat_scr[pl.ds(c * B * pitch + h, B, stride=pitch), :] = even
                flat_scr[pl.ds(c * B * pitch + h + 1, B, stride=pitch), :] = odd
    for c in range(nLT):
        for b in range(B):
            y_ref[b, :, c * LT:(c + 1) * LT] = flat_scr[(c * B + b) * pitch:(c * B + b) * pitch + Hc, :].astype(y_ref.dtype)


def _ssm_pack(lambda_re, lambda_im, log_dt, b_re, b_im, c_re, c_im):
    G, P = lambda_re.shape
    Hc = SSM_GROUP
    assert 2 * P == 128 and 3 * Hc <= 128
    lanes = lambda a: jnp.pad(a, ((0, 0), (0, 0), (0, 128 - a.shape[2])))
    head = jnp.pad(jnp.stack([lanes(lambda_re[:, None, :])[:, 0], lanes(lambda_im[:, None, :])[:, 0],
                              lanes(log_dt[:, None, None])[:, 0]], axis=1), ((0, 0), (0, 5), (0, 0)))
    b_t = jnp.concatenate([jnp.swapaxes(b_re, 1, 2), jnp.swapaxes(b_im, 1, 2)], axis=2)
    c_n = jnp.concatenate([c_re, c_im], axis=2)
    cols = jnp.concatenate([lambda_re[:, :, None], lambda_im[:, :, None], jnp.zeros((G, P, Hc - 2), f32),
                            jnp.swapaxes(c_re, 1, 2), jnp.swapaxes(c_im, 1, 2)], axis=2)
    return jnp.concatenate([head, b_t, c_n, lanes(cols)], axis=1)


def _ssm(u_t, lambda_re, lambda_im, log_dt, b_re, b_im, c_re, c_im, d_skip):
    B, width, L = u_t.shape
    G, P = lambda_re.shape
    Hc, T = SSM_GROUP, SSM_CHUNK
    assert width == G * Hc and P == SSM_STATE and L % T == 0
    nC = L // T
    params = _ssm_pack(lambda_re, lambda_im, log_dt, b_re, b_im, c_re, c_im)
    d_exp = jnp.repeat(d_skip.reshape(G, 1, Hc), T, axis=2)
    g3 = lambda shp: pl.BlockSpec((1,) + shp, lambda g: (g, 0, 0))
    u_spec = pl.BlockSpec((B, Hc, L), lambda g: (0, g, 0))
    return pl.pallas_call(
        _ssm_kernel,
        grid=(G,),
        in_specs=[u_spec, g3(params.shape[1:]), g3((1, Hc * T))],
        out_specs=u_spec,
        out_shape=jax.ShapeDtypeStruct((B, width, L), bf16),
        scratch_shapes=[
            pltpu.VMEM((nC * B, Hc * T), bf16),
            pltpu.VMEM((Hc * T, Hc * T), bf16),
            pltpu.VMEM((Hc * T, 2 * P), bf16),
            pltpu.VMEM((2 * P, Hc * T), bf16),
            pltpu.VMEM((nC * B, 2 * P), bf16),
            pltpu.VMEM((L // (2 * T) * B * SSM_ROW_PITCH, 2 * T), f32),
        ],
        compiler_params=_params(("arbitrary",)),
        name="ssm",
    )(u_t, params, d_exp)


MIX_CHUNK = 256


def _mix_kernel(x_ref, a_ref, y_ref, wg_ref, bg_ref, ga_ref, gs_ref, wo_ref, o_ref):
    def rms_rows(t, g_ref):
        return t * lax.rsqrt(jnp.mean(t * t, axis=0, keepdims=True) + EPS) * g_ref[...]

    for c in range(0, x_ref.shape[1], MIX_CHUNK):
        tok = slice(c, c + MIX_CHUNK)
        a_n = rms_rows(a_ref[0, :, tok].astype(f32), ga_ref)
        yg = y_ref[0, :, tok]
        z = jnp.dot(wg_ref[...], yg, preferred_element_type=f32) + bg_ref[...]
        s_n = rms_rows(yg.astype(f32) * jax.nn.sigmoid(z), gs_ref)
        mixed_t = jnp.concatenate([a_n, s_n], axis=0).astype(bf16)
        delta = lax.dot_general(mixed_t, wo_ref[...], (((0,), (0,)), ((), ())), preferred_element_type=f32)
        o_ref[0, tok, :] = x_ref[0, tok, :] + delta


def _mix(x, attn_t, yg_t, w_glu_t, b_glu, g_attn, g_ssm, w_out, tn=1024):
    B, L, D = x.shape
    aw, sw = attn_t.shape[1], yg_t.shape[1]
    assert aw + sw == D and L % tn == 0 and tn % MIX_CHUNK == 0
    const = lambda shp: pl.BlockSpec(shp, lambda b, i: (0, 0), pipeline_mode=pl.Buffered(1))
    return pl.pallas_call(
        _mix_kernel,
        grid=(B, L // tn),
        in_specs=[
            pl.BlockSpec((1, tn, D), lambda b, i: (b, i, 0)),
            pl.BlockSpec((1, aw, tn), lambda b, i: (b, 0, i)),
            pl.BlockSpec((1, sw, tn), lambda b, i: (b, 0, i)),
            const((sw, sw)), const((sw, 1)), const((aw, 1)), const((sw, 1)), const((D, D)),
        ],
        out_specs=pl.BlockSpec((1, tn, D), lambda b, i: (b, i, 0)),
        out_shape=jax.ShapeDtypeStruct((B, L, D), f32),
        compiler_params=_params(("arbitrary", "arbitrary")),
        name="mix",
    )(x, attn_t, yg_t, w_glu_t, b_glu, g_attn, g_ssm, w_out)


def _ffn_kernel(x_ref, g2_ref, w1_ref, w2_ref, o_ref, h_scr):
    f = pl.program_id(1)

    @pl.when(f == 0)
    def _():
        x = x_ref[...]
        ms = jnp.mean(x * x, axis=-1, keepdims=True)
        h_scr[...] = (x * lax.rsqrt(ms + EPS) * g2_ref[...]).astype(bf16)
        o_ref[...] = x

    a = jnp.dot(h_scr[...], w1_ref[...], preferred_element_type=f32)
    a = jnp.square(jnp.maximum(a, 0.0)).astype(bf16)
    o_ref[...] += jnp.dot(a, w2_ref[...], preferred_element_type=f32)


def _ffn(x2d, g2, w1, w2, tm=512, tf=1024):
    N, D = x2d.shape
    F = w1.shape[1]
    assert N % tm == 0 and F % tf == 0
    return pl.pallas_call(
        _ffn_kernel,
        grid=(N // tm, F // tf),
        in_specs=[
            pl.BlockSpec((tm, D), lambda i, f: (i, 0)),
            pl.BlockSpec((1, D), lambda i, f: (0, 0)),
            pl.BlockSpec((D, tf), lambda i, f: (0, f)),
            pl.BlockSpec((tf, D), lambda i, f: (f, 0)),
        ],
        out_specs=pl.BlockSpec((tm, D), lambda i, f: (i, 0)),
        out_shape=jax.ShapeDtypeStruct((N, D), f32),
        scratch_shapes=[pltpu.VMEM((tm, D), bf16)],
        compiler_params=_params(("arbitrary", "arbitrary")),
        name="ffn",
    )(x2d, g2, w1, w2)


def _layer(x, norm1_gain, w_in, q_norm_gain, k_norm_gain, attn_out_gain, lambda_re, lambda_im, log_dt,
           b_re, b_im, c_re, c_im, d_skip, w_glu, b_glu, ssm_out_gain, w_out, norm2_gain, w_ff1, w_ff2):
    B, L, D = x.shape
    sw = d_skip.shape[0]
    aw = D - sw
    assert aw == sw and w_in.shape == (D, 3 * aw + sw)
    q_t, k_t, v_t, u_t = _in_proj(
        x, norm1_gain.reshape(1, D), w_in.astype(bf16),
        q_norm_gain.reshape(HEAD_DIM, 1), k_norm_gain.reshape(HEAD_DIM, 1), aw)
    attn_t = _moba(q_t, k_t, v_t, q_norm_gain, k_norm_gain)
    yg_t = _ssm(u_t, lambda_re, lambda_im, log_dt, b_re, b_im, c_re, c_im, d_skip)
    x1 = _mix(x, attn_t, yg_t, w_glu.T.astype(bf16), b_glu.reshape(sw, 1),
              attn_out_gain.reshape(aw, 1), ssm_out_gain.reshape(sw, 1), w_out.astype(bf16))
    out = _ffn(x1.reshape(B * L, D), norm2_gain.reshape(1, D), w_ff1.astype(bf16), w_ff2.astype(bf16))
    return out.reshape(B, L, D)


def kernel(x, norm1_gain, w_in, q_norm_gain, k_norm_gain, attn_out_gain, lambda_re, lambda_im, log_dt, b_re, b_im, c_re, c_im, d_skip, w_glu, b_glu, ssm_out_gain, w_out, norm2_gain, w_ff1, w_ff2):
    per_layer = (norm1_gain, w_in, q_norm_gain, k_norm_gain, attn_out_gain, lambda_re, lambda_im, log_dt,
                 b_re, b_im, c_re, c_im, d_skip, w_glu, b_glu, ssm_out_gain, w_out, norm2_gain, w_ff1, w_ff2)
    for i in range(w_in.shape[0]):
        x = _layer(x, *(p[i] for p in per_layer))
    return x
```

```python
import functools
import math

import jax
import jax.numpy as jnp
from jax import lax
from jax.experimental import pallas as pl
from jax.experimental.pallas import tpu as pltpu

f32 = jnp.float32
bf16 = jnp.bfloat16

EPS = 1e-6
NEG = -1e30
HEAD_DIM = 64
MOBA_BLOCK = 256
MOBA_TOPK = 3
SSM_GROUP = 16
SSM_STATE = 64
SSM_CHUNK = 64
VMEM_LIMIT = 56 * 1024 * 1024
HIGHEST = lax.Precision.HIGHEST


def _params(sem):
    return pltpu.CompilerParams(dimension_semantics=sem, vmem_limit_bytes=VMEM_LIMIT)


def _in_proj_kernel(x_ref, g1_ref, w_ref, gq_ref, gk_ref, q_ref, k_ref, v_ref, u_ref, h_scr):
    j = pl.program_id(1)
    i = pl.program_id(2)
    tm = x_ref.shape[1]

    @pl.when(j == 0)
    def _():
        x = x_ref[0]
        ms = jnp.mean(x * x, axis=-1, keepdims=True)
        h_scr[i] = (x * lax.rsqrt(ms + EPS) * g1_ref[...]).astype(bf16)

    def project(out_ref, finish):
        t = lax.dot_general(w_ref[...], h_scr[i], (((0,), (1,)), ((), ())), preferred_element_type=f32)
        out_ref[0] = finish(t)

    def head_norm(g_ref):
        def finish(t):
            t3 = t.reshape(t.shape[0] // HEAD_DIM, HEAD_DIM, tm)
            ms = jnp.mean(t3 * t3, axis=1, keepdims=True)
            return (t3 * lax.rsqrt(ms + EPS) * g_ref[...][None]).reshape(t.shape).astype(bf16)
        return finish

    @pl.when(j == 0)
    def _():
        project(q_ref, head_norm(gq_ref))

    @pl.when(j == 1)
    def _():
        project(k_ref, head_norm(gk_ref))

    @pl.when(j == 2)
    def _():
        project(v_ref, lambda t: t.astype(bf16))

    @pl.when(j == 3)
    def _():
        project(u_ref, lambda t: t.astype(bf16))


def _in_proj(x, g1, w_in, gq, gk, width, tm=512):
    B, L, D = x.shape
    assert w_in.shape == (D, 4 * width) and L % tm == 0
    n_i = L // tm
    last = n_i - 1

    def out_spec(jj):
        return pl.BlockSpec((1, width, tm),
                            lambda b, j, i: (b, 0, jnp.where(j == jj, i, jnp.where(j < jj, 0, last))))

    return pl.pallas_call(
        _in_proj_kernel,
        grid=(B, 4, n_i),
        in_specs=[
            pl.BlockSpec((1, tm, D), lambda b, j, i: (b, jnp.where(j == 0, i, last), 0)),
            pl.BlockSpec((1, D), lambda b, j, i: (0, 0)),
            pl.BlockSpec((D, width), lambda b, j, i: (0, j)),
            pl.BlockSpec((HEAD_DIM, 1), lambda b, j, i: (0, 0)),
            pl.BlockSpec((HEAD_DIM, 1), lambda b, j, i: (0, 0)),
        ],
        out_specs=[out_spec(0), out_spec(1), out_spec(2), out_spec(3)],
        out_shape=[jax.ShapeDtypeStruct((B, width, L), bf16)] * 4,
        scratch_shapes=[pltpu.VMEM((n_i, tm, D), bf16)],
        compiler_params=_params(("arbitrary", "arbitrary", "arbitrary")),
        name="in_proj",
    )(x, g1, w_in, gq, gk)


LOG2E = 1.4426950408889634
V_AUG_ROWS = HEAD_DIM + 16
MOBA_PIPE_DEPTH = 5


def _moba_kernel(sp_ref, q_ref, k_ref, v_ref, o_ref, ktok_scr, vaug_scr, s_scr, *, single_pass):
    L = q_ref.shape[2]
    blk = MOBA_BLOCK
    nb = L // blk
    top_k = min(MOBA_TOPK, nb - 1)
    qscale = HEAD_DIM ** -0.5 * LOG2E
    pair = pl.program_id(1)

    r_i = lax.broadcasted_iota(jnp.int32, (blk, blk), 0)
    c_i = lax.broadcasted_iota(jnp.int32, (blk, blk), 1)
    causal = r_i <= c_i
    blk_id = lax.broadcasted_iota(jnp.int32, (nb, blk), 0)
    q_row = lax.broadcasted_iota(jnp.int32, (HEAD_DIM, blk), 0)
    q_off = lax.broadcasted_iota(jnp.int32, (1, blk), 1).astype(f32)

    k_t = k_ref[0].astype(f32)
    ktok = k_t.T
    kmean = jnp.concatenate(
        [jnp.mean(ktok[j * blk:(j + 1) * blk], axis=0, keepdims=True) for j in range(nb)], axis=0)
    lane = lax.broadcasted_iota(jnp.int32, ktok.shape, 1)
    key_off = jnp.bitwise_and(lax.broadcasted_iota(jnp.int32, ktok.shape, 0), blk - 1).astype(f32)
    ones_rows = jnp.where(lax.broadcasted_iota(jnp.int32, (V_AUG_ROWS - HEAD_DIM, L), 0) == 0, 1.0, 0.0).astype(bf16)
    for hh in range(2):
        kh = ktok if hh == 0 else pltpu.roll(ktok, HEAD_DIM, 1)
        ktok_scr[hh] = jnp.where(lane < HEAD_DIM, kh, jnp.where(lane < HEAD_DIM + 3, key_off, 0.0)).astype(bf16)
        vaug_scr[hh, 0:HEAD_DIM, :] = v_ref[0, hh * HEAD_DIM:(hh + 1) * HEAD_DIM, :]
        vaug_scr[hh, HEAD_DIM:, :] = ones_rows

    @functools.cache
    def head_setup(hh):
        rows = slice(hh * HEAD_DIM, (hh + 1) * HEAD_DIM)
        head = pair * 2 + hh
        pieces = [jnp.full((HEAD_DIM, blk), sp_ref[head, t], f32) for t in range(3)]
        q_extra = jnp.where(q_row == 0, pieces[0], jnp.where(q_row == 1, pieces[1],
                            jnp.where(q_row == 2, pieces[2], 0.0))).astype(bf16)
        slope2 = sum(jnp.full((1, blk), sp_ref[head, t], f32) for t in range(3))
        kmax = []
        if single_pass:
            kh_t = k_t[rows, :]
            kn2 = jnp.sum(kh_t * kh_t, axis=0, keepdims=True)
            for j in range(nb):
                bj = jnp.sqrt(jnp.max(kn2[:, j * blk:(j + 1) * blk], axis=1, keepdims=True))
                kmax.append(bj if j == 0 else jnp.maximum(kmax[-1], bj))
        return rows, q_extra, slope2, kmax

    @functools.cache
    def block_setup(hh, i):
        rows, q_extra, slope2, kmax = head_setup(hh)
        qf = q_ref[0, rows, i * blk:(i + 1) * blk].astype(f32)
        q_aug = jnp.concatenate([(qf * qscale).astype(bf16), q_extra], axis=0)

        def scores(j):
            s = jnp.dot(ktok_scr[hh, j * blk:(j + 1) * blk, :], q_aug, preferred_element_type=f32)
            return jnp.where(causal, s, NEG) if j == i else s

        sel_off = []
        if i > 0:
            z = jnp.zeros_like(qf)
            q_pair = jnp.concatenate([qf, z] if hh == 0 else [z, qf], axis=0)
            gate = jnp.dot(kmean, q_pair, precision=HIGHEST, preferred_element_type=f32)
            rank = jnp.zeros((nb, blk), f32)
            for jp in range(i):
                row = gate[jp:jp + 1, :]
                beats = (row > gate) | ((row == gate) & (jp < blk_id))
                rank = rank + jnp.where(beats, 1.0, 0.0)
            sel = jnp.where((blk_id < i) & (rank < top_k), 1.0, 0.0)
            sel_off = [(sel[j:j + 1, :] > 0.5, slope2 * float((j - i) * blk)) for j in range(i)]
        shifts = None
        if single_pass:
            q_norm = jnp.sqrt(jnp.sum(qf * qf, axis=0, keepdims=True))
            m = (qscale * q_norm) * kmax[i] + slope2 * q_off
            shifts = [jnp.where(selj, cj - m, NEG) for selj, cj in sel_off] + [-m]
        return scores, sel_off, shifts

    def finish(hh, i, acc):
        rows = head_setup(hh)[0]
        o_ref[0, rows, i * blk:(i + 1) * blk] = (acc[0:HEAD_DIM] / acc[HEAD_DIM:HEAD_DIM + 1]).astype(o_ref.dtype)

    def weighted_values(hh, j, s, shift):
        p = jnp.exp2(s + shift).astype(bf16)
        return jnp.dot(vaug_scr[hh, :, j * blk:(j + 1) * blk], p, preferred_element_type=f32)

    if single_pass:
        order = [(hh, i, j) for hh in range(2) for i in range(nb) for j in range(i + 1)]
        depth = MOBA_PIPE_DEPTH

        def issue_scores(n):
            hh, i, j = order[n]
            s_scr[n % (depth + 1)] = block_setup(hh, i)[0](j)

        for n in range(depth):
            issue_scores(n)
        acc = None
        for n, (hh, i, j) in enumerate(order):
            if n + depth < len(order):
                issue_scores(n + depth)
            d = weighted_values(hh, j, s_scr[n % (depth + 1)], block_setup(hh, i)[2][j])
            acc = d if j == 0 else acc + d
            if j == i:
                finish(hh, i, acc)
    else:
        for hh in range(2):
            for i in range(nb):
                scores, sel_off, _ = block_setup(hh, i)
                s = [scores(j) for j in range(i + 1)]
                m = jnp.max(s[i], axis=0, keepdims=True)
                for j in range(i):
                    selj, cj = sel_off[j]
                    m = jnp.maximum(m, jnp.where(selj, jnp.max(s[j], axis=0, keepdims=True) + cj, NEG))
                shifts = [jnp.where(selj, cj - m, NEG) for selj, cj in sel_off] + [-m]
                acc = None
                for j in range(i + 1):
                    d = weighted_values(hh, j, s[j], shifts[j])
                    acc = d if acc is None else acc + d
                finish(hh, i, acc)


SINGLE_PASS_MAX_GAP = 64.0


def _moba(q_t, k_t, v_t, gq, gk):
    B, width, L = q_t.shape
    n_heads = width // HEAD_DIM
    assert n_heads % 2 == 0 and L % MOBA_BLOCK == 0 and MOBA_BLOCK & (MOBA_BLOCK - 1) == 0
    slope2 = jnp.exp2(-8.0 * (jnp.arange(n_heads, dtype=f32) + 1.0) / n_heads) * LOG2E
    hi = slope2.astype(bf16).astype(f32)
    mid = (slope2 - hi).astype(bf16).astype(f32)
    lo = (slope2 - hi - mid).astype(bf16).astype(f32)
    pieces = jnp.stack([hi, mid, lo], axis=1)
    spec = pl.BlockSpec((1, 2 * HEAD_DIM, L), lambda b, h: (b, h, 0))

    def call(single_pass):
        return pl.pallas_call(
            functools.partial(_moba_kernel, single_pass=single_pass),
            grid=(B, n_heads // 2),
            in_specs=[pl.BlockSpec(memory_space=pltpu.SMEM), spec, spec, spec],
            out_specs=spec,
            out_shape=jax.ShapeDtypeStruct((B, width, L), bf16),
            scratch_shapes=[
                pltpu.VMEM((2, L, 2 * HEAD_DIM), bf16),
                pltpu.VMEM((2, V_AUG_ROWS, L), bf16),
                pltpu.VMEM((MOBA_PIPE_DEPTH + 1, MOBA_BLOCK, MOBA_BLOCK), f32),
            ],
            compiler_params=_params(("arbitrary", "arbitrary")),
            name="moba_single_pass" if single_pass else "moba_two_pass",
        )(pieces, q_t, k_t, v_t)

    gap = 2.0 * (HEAD_DIM ** -0.5 * LOG2E) * HEAD_DIM * jnp.max(jnp.abs(gq)) * jnp.max(jnp.abs(gk))
    return lax.cond(gap <= SINGLE_PASS_MAX_GAP, lambda: call(True), lambda: call(False))


SSM_ROW_PITCH = 24


def _gelu_tanh(x):
    c = math.sqrt(2.0 / math.pi)
    w = x * ((-2.0 * c * 0.044715 * LOG2E) * (x * x) + (-2.0 * c * LOG2E))
    return x / (1.0 + jnp.exp2(w))


def _ssm_kernel(u_ref, par_ref, dexp_ref, y_ref, lhs_scr, m_scr, ws_scr, wo_scr, sp_scr, flat_scr):
    B = u_ref.shape[0]
    L = u_ref.shape[2]
    T = SSM_CHUNK
    LT = 2 * T
    P = SSM_STATE
    Hc = SSM_GROUP
    nC = L // T
    nLT = L // LT
    pitch = SSM_ROW_PITCH
    lane = lax.broadcasted_iota(jnp.int32, (B, LT), 1)

    def regroup(a, b):
        low = lane[:a.shape[0]] < T
        swapped = pltpu.roll(jnp.where(low, b, a), T, 1)
        return jnp.where(low, a, swapped), jnp.where(low, swapped, b)

    for c in range(nLT):
        for b in range(B):
            flat_scr[(c * B + b) * pitch:(c * B + b) * pitch + Hc, :] = u_ref[b, :, c * LT:(c + 1) * LT].astype(f32)
    for c in range(nLT):
        for hp in range(0, Hc, 2):
            ue = flat_scr[pl.ds(c * B * pitch + hp, B, stride=pitch), :]
            uo = flat_scr[pl.ds(c * B * pitch + hp + 1, B, stride=pitch), :]
            first, second = regroup(ue, uo)
            lhs_scr[(2 * c) * B:(2 * c + 1) * B, hp * T:(hp + 2) * T] = first.astype(bf16)
            lhs_scr[(2 * c + 1) * B:(2 * c + 2) * B, hp * T:(hp + 2) * T] = second.astype(bf16)

    P2 = 2 * P
    lr_r, li_r = par_ref[0, 0:1, 0:P], par_ref[0, 1:2, 0:P]
    dt = jnp.exp(par_ref[0, 2:3, 0:1])
    bt_re, bt_im = par_ref[0, 8:8 + Hc, 0:P], par_ref[0, 8:8 + Hc, P:P2]
    c_re, c_im = par_ref[0, 8 + Hc:8 + 2 * Hc, 0:P], par_ref[0, 8 + Hc:8 + 2 * Hc, P:P2]
    col0 = 8 + 2 * Hc
    lr_c, li_c = par_ref[0, col0:col0 + P, 0:1], par_ref[0, col0:col0 + P, 1:2]
    ct_re, ct_im = par_ref[0, col0:col0 + P, Hc:2 * Hc], par_ref[0, col0:col0 + P, 2 * Hc:3 * Hc]

    def lam_pow(lr, li, n):
        mag = jnp.exp(n * (lr * dt))
        ang = n * (li * dt)
        return mag * jnp.cos(ang), mag * jnp.sin(ang)

    ab_re, ab_im = lam_pow(lr_r, li_r, 1.0)
    den = lr_r * lr_r + li_r * li_r
    nr = ab_re - 1.0
    f_re = (nr * lr_r + ab_im * li_r) / den
    f_im = (ab_im * lr_r - nr * li_r) / den
    bb_re = f_re * bt_re - f_im * bt_im
    bb_im = f_re * bt_im + f_im * bt_re

    d_lane = jnp.bitwise_and(lax.broadcasted_iota(jnp.int32, (P, LT), 1), T - 1).astype(f32)
    pd_re, pd_im = lam_pow(lr_c, li_c, d_lane)

    cb_re = (c_re[:, None, :] * bb_re[None, :, :] - c_im[:, None, :] * bb_im[None, :, :]).reshape(Hc * Hc, P)
    cb_im = (c_re[:, None, :] * bb_im[None, :, :] + c_im[:, None, :] * bb_re[None, :, :]).reshape(Hc * Hc, P)
    kt = (jnp.dot(cb_re, pd_re, precision=HIGHEST, preferred_element_type=f32)
          - jnp.dot(cb_im, pd_im, precision=HIGHEST, preferred_element_type=f32))
    kt4 = kt.reshape(Hc // 2, 2, Hc, LT)
    k_even, k_odd = kt4[:, 0].reshape(Hc // 2 * Hc, LT), kt4[:, 1].reshape(Hc // 2 * Hc, LT)
    lane_k = lax.broadcasted_iota(jnp.int32, k_even.shape, 1)
    kp = jnp.where(lane_k < T, k_even, k_odd)

    flip = jnp.where(lax.broadcasted_iota(jnp.int32, (T, LT), 0) + lax.broadcasted_iota(jnp.int32, (T, LT), 1) == T - 1,
                     1.0, 0.0)
    rev_t = lambda a: lax.dot_general(flip, a, (((1,), (1,)), ((), ())), precision=HIGHEST, preferred_element_type=f32)
    pe_re, pe_im = rev_t(pd_re), rev_t(pd_im)
    for hp in range(Hc):
        br, bi = bb_re[hp:hp + 1, :], bb_im[hp:hp + 1, :]
        ws_scr[hp * T:(hp + 1) * T, :] = jnp.concatenate(
            [pe_re * br - pe_im * bi, pe_re * bi + pe_im * br], axis=1).astype(bf16)
    l1_re, l1_im = pd_re[:, 1:2], pd_im[:, 1:2]
    po_re, po_im = pd_re * l1_re - pd_im * l1_im, pd_re * l1_im + pd_im * l1_re
    lane_p = lax.broadcasted_iota(jnp.int32, (2 * P, LT), 1)

    def wo_block(h):
        cre, cim = ct_re[:, h:h + 1], ct_im[:, h:h + 1]
        return jnp.concatenate([cre * po_re - cim * po_im, -(cre * po_im + cim * po_re)], axis=0)

    for h in range(0, Hc, 2):
        wo_scr[:, h * T:(h + 2) * T] = jnp.where(lane_p < T, wo_block(h), wo_block(h + 1)).astype(bf16)

    v = jnp.dot(lhs_scr[...], ws_scr[...], preferred_element_type=f32)
    a_re, a_im = lam_pow(lr_r, li_r, float(T))
    a1 = jnp.concatenate([a_re, a_re], axis=1)
    a2 = jnp.concatenate([-a_im, a_im], axis=1)
    v_sw = pltpu.roll(v, P, 1)
    s = jnp.zeros((B, 2 * P), f32)
    s_sw = jnp.zeros((B, 2 * P), f32)
    for c in range(nC):
        sp_scr[c * B:(c + 1) * B, :] = s.astype(bf16)
        s, s_sw = (a1 * s + a2 * s_sw + v[c * B:(c + 1) * B, :],
                   a1 * s_sw - a2 * s + v_sw[c * B:(c + 1) * B, :])

    tau_i = lax.broadcasted_iota(jnp.int32, (T, LT), 0)
    t_i = jnp.bitwise_and(lax.broadcasted_iota(jnp.int32, (T, LT), 1), T - 1)
    causal = t_i >= tau_i

    for h0 in range(0, Hc, 4):
        cols = slice(h0 * T, (h0 + 4) * T)
        for h in (h0, h0 + 2):
            for hp in range(Hc):
                row = kp[(h // 2) * Hc + hp:(h // 2) * Hc + hp + 1, :]
                tile = pltpu.roll(jnp.broadcast_to(row, (T, LT)), 0, 1, stride=1, stride_axis=0)
                m_scr[hp * T:(hp + 1) * T, h * T:(h + 2) * T] = jnp.where(causal, tile, 0.0).astype(bf16)
        lhs4 = lhs_scr[:, cols]
        y = jnp.dot(lhs_scr[...], m_scr[:, cols], preferred_element_type=f32)
        y = y + jnp.dot(sp_scr[...], wo_scr[:, cols], preferred_element_type=f32)
        y = _gelu_tanh(y + dexp_ref[0, :, cols] * lhs4.astype(f32))
        for c in range(nLT):
            for k, h in enumerate((h0, h0 + 2)):
                y0 = y[(2 * c) * B:(2 * c + 1) * B, k * LT:(k + 1) * LT]
                y1 = y[(2 * c + 1) * B:(2 * c + 2) * B, k * LT:(k + 1) * LT]
                even, odd = regroup(y0, y1)
                flat_scr[pl.ds(c * B * pitch + h, B, stride=pitch), :] = even
                flat_scr[pl.ds(c * B * pitch + h + 1, B, stride=pitch), :] = odd
    for c in range(nLT):
        for b in range(B):
            y_ref[b, :, c * LT:(c + 1) * LT] = flat_scr[(c * B + b) * pitch:(c * B + b) * pitch + Hc, :].astype(y_ref.dtype)


def _ssm_pack(lambda_re, lambda_im, log_dt, b_re, b_im, c_re, c_im):
    G, P = lambda_re.shape
    Hc = SSM_GROUP
    assert 2 * P == 128 and 3 * Hc <= 128
    lanes = lambda a: jnp.pad(a, ((0, 0), (0, 0), (0, 128 - a.shape[2])))
    head = jnp.pad(jnp.stack([lanes(lambda_re[:, None, :])[:, 0], lanes(lambda_im[:, None, :])[:, 0],
                              lanes(log_dt[:, None, None])[:, 0]], axis=1), ((0, 0), (0, 5), (0, 0)))
    b_t = jnp.concatenate([jnp.swapaxes(b_re, 1, 2), jnp.swapaxes(b_im, 1, 2)], axis=2)
    c_n = jnp.concatenate([c_re, c_im], axis=2)
    cols = jnp.concatenate([lambda_re[:, :, None], lambda_im[:, :, None], jnp.zeros((G, P, Hc - 2), f32),
                            jnp.swapaxes(c_re, 1, 2), jnp.swapaxes(c_im, 1, 2)], axis=2)
    return jnp.concatenate([head, b_t, c_n, lanes(cols)], axis=1)


def _ssm(u_t, lambda_re, lambda_im, log_dt, b_re, b_im, c_re, c_im, d_skip):
    B, width, L = u_t.shape
    G, P = lambda_re.shape
    Hc, T = SSM_GROUP, SSM_CHUNK
    assert width == G * Hc and P == SSM_STATE and L % T == 0
    nC = L // T
    params = _ssm_pack(lambda_re, lambda_im, log_dt, b_re, b_im, c_re, c_im)
    d_exp = jnp.repeat(d_skip.reshape(G, 1, Hc), T, axis=2)
    g3 = lambda shp: pl.BlockSpec((1,) + shp, lambda g: (g, 0, 0))
    u_spec = pl.BlockSpec((B, Hc, L), lambda g: (0, g, 0))
    return pl.pallas_call(
        _ssm_kernel,
        grid=(G,),
        in_specs=[u_spec, g3(params.shape[1:]), g3((1, Hc * T))],
        out_specs=u_spec,
        out_shape=jax.ShapeDtypeStruct((B, width, L), bf16),
        scratch_shapes=[
            pltpu.VMEM((nC * B, Hc * T), bf16),
            pltpu.VMEM((Hc * T, Hc * T), bf16),
            pltpu.VMEM((Hc * T, 2 * P), bf16),
            pltpu.VMEM((2 * P, Hc * T), bf16),
            pltpu.VMEM((nC * B, 2 * P), bf16),
            pltpu.VMEM((L // (2 * T) * B * SSM_ROW_PITCH, 2 * T), f32),
        ],
        compiler_params=_params(("arbitrary",)),
        name="ssm",
    )(u_t, params, d_exp)


MIX_CHUNK = 256


def _mix_kernel(x_ref, a_ref, y_ref, wg_ref, bg_ref, ga_ref, gs_ref, wo_ref, o_ref):
    def rms_rows(t, g_ref):
        return t * lax.rsqrt(jnp.mean(t * t, axis=0, keepdims=True) + EPS) * g_ref[...]

    for c in range(0, x_ref.shape[1], MIX_CHUNK):
        tok = slice(c, c + MIX_CHUNK)
        a_n = rms_rows(a_ref[0, :, tok].astype(f32), ga_ref)
        yg = y_ref[0, :, tok]
        z = jnp.dot(wg_ref[...], yg, preferred_element_type=f32) + bg_ref[...]
        s_n = rms_rows(yg.astype(f32) * jax.nn.sigmoid(z), gs_ref)
        mixed_t = jnp.concatenate([a_n, s_n], axis=0).astype(bf16)
        delta = lax.dot_general(mixed_t, wo_ref[...], (((0,), (0,)), ((), ())), preferred_element_type=f32)
        o_ref[0, tok, :] = x_ref[0, tok, :] + delta


def _mix(x, attn_t, yg_t, w_glu_t, b_glu, g_attn, g_ssm, w_out, tn=1024):
    B, L, D = x.shape
    aw, sw = attn_t.shape[1], yg_t.shape[1]
    assert aw + sw == D and L % tn == 0 and tn % MIX_CHUNK == 0
    const = lambda shp: pl.BlockSpec(shp, lambda b, i: (0, 0), pipeline_mode=pl.Buffered(1))
    return pl.pallas_call(
        _mix_kernel,
        grid=(B, L // tn),
        in_specs=[
            pl.BlockSpec((1, tn, D), lambda b, i: (b, i, 0)),
            pl.BlockSpec((1, aw, tn), lambda b, i: (b, 0, i)),
            pl.BlockSpec((1, sw, tn), lambda b, i: (b, 0, i)),
            const((sw, sw)), const((sw, 1)), const((aw, 1)), const((sw, 1)), const((D, D)),
        ],
        out_specs=pl.BlockSpec((1, tn, D), lambda b, i: (b, i, 0)),
        out_shape=jax.ShapeDtypeStruct((B, L, D), f32),
        compiler_params=_params(("arbitrary", "arbitrary")),
        name="mix",
    )(x, attn_t, yg_t, w_glu_t, b_glu, g_attn, g_ssm, w_out)


def _ffn_kernel(x_ref, g2_ref, w1_ref, w2_ref, o_ref, h_scr):
    f = pl.program_id(1)

    @pl.when(f == 0)
    def _():
        x = x_ref[...]
        ms = jnp.mean(x * x, axis=-1, keepdims=True)
        h_scr[...] = (x * lax.rsqrt(ms + EPS) * g2_ref[...]).astype(bf16)
        o_ref[...] = x

    a = jnp.dot(h_scr[...], w1_ref[...], preferred_element_type=f32)
    a = jnp.square(jnp.maximum(a, 0.0)).astype(bf16)
    o_ref[...] += jnp.dot(a, w2_ref[...], preferred_element_type=f32)


def _ffn(x2d, g2, w1, w2, tm=512, tf=1024):
    N, D = x2d.shape
    F = w1.shape[1]
    assert N % tm == 0 and F % tf == 0
    return pl.pallas_call(
        _ffn_kernel,
        grid=(N // tm, F // tf),
        in_specs=[
            pl.BlockSpec((tm, D), lambda i, f: (i, 0)),
            pl.BlockSpec((1, D), lambda i, f: (0, 0)),
            pl.BlockSpec((D, tf), lambda i, f: (0, f)),
            pl.BlockSpec((tf, D), lambda i, f: (f, 0)),
        ],
        out_specs=pl.BlockSpec((tm, D), lambda i, f: (i, 0)),
        out_shape=jax.ShapeDtypeStruct((N, D), f32),
        scratch_shapes=[pltpu.VMEM((tm, D), bf16)],
        compiler_params=_params(("arbitrary", "arbitrary")),
        name="ffn",
    )(x2d, g2, w1, w2)


def _layer(x, norm1_gain, w_in, q_norm_gain, k_norm_gain, attn_out_gain, lambda_re, lambda_im, log_dt,
           b_re, b_im, c_re, c_im, d_skip, w_glu, b_glu, ssm_out_gain, w_out, norm2_gain, w_ff1, w_ff2):
    B, L, D = x.shape
    sw = d_skip.shape[0]
    aw = D - sw
    assert aw == sw and w_in.shape == (D, 3 * aw + sw)
    q_t, k_t, v_t, u_t = _in_proj(
        x, norm1_gain.reshape(1, D), w_in.astype(bf16),
        q_norm_gain.reshape(HEAD_DIM, 1), k_norm_gain.reshape(HEAD_DIM, 1), aw)
    attn_t = _moba(q_t, k_t, v_t, q_norm_gain, k_norm_gain)
    yg_t = _ssm(u_t, lambda_re, lambda_im, log_dt, b_re, b_im, c_re, c_im, d_skip)
    x1 = _mix(x, attn_t, yg_t, w_glu.T.astype(bf16), b_glu.reshape(sw, 1),
              attn_out_gain.reshape(aw, 1), ssm_out_gain.reshape(sw, 1), w_out.astype(bf16))
    out = _ffn(x1.reshape(B * L, D), norm2_gain.reshape(1, D), w_ff1.astype(bf16), w_ff2.astype(bf16))
    return out.reshape(B, L, D)


def kernel(x, norm1_gain, w_in, q_norm_gain, k_norm_gain, attn_out_gain, lambda_re, lambda_im, log_dt, b_re, b_im, c_re, c_im, d_skip, w_glu, b_glu, ssm_out_gain, w_out, norm2_gain, w_ff1, w_ff2):
    per_layer = (norm1_gain, w_in, q_norm_gain, k_norm_gain, attn_out_gain, lambda_re, lambda_im, log_dt,
                 b_re, b_im, c_re, c_im, d_skip, w_glu, b_glu, ssm_out_gain, w_out, norm2_gain, w_ff1, w_ff2)
    for i in range(w_in.shape[0]):
        x = _layer(x, *(p[i] for p in per_layer))
    return x
```
